```python
import jax, jax.numpy as jnp
from jax import lax
import numpy as np

D_MODEL = 4096
BATCH = 2
SEQ = 8192
DEPTH = 2

HG_DK = 128
HG_DV = 128
HG_HEADS = (D_MODEL // 2) // HG_DV
GD_DK = 128
GD_DV = 128
GD_HEADS = (D_MODEL // 2) // GD_DV
HG_KEY = HG_HEADS * HG_DK
HG_WIDTH = HG_HEADS * HG_DV
GD_KEY = GD_HEADS * GD_DK
GD_WIDTH = GD_HEADS * GD_DV
MIX_WIDTH = HG_WIDTH + GD_WIDTH
CHUNK = 64
CONV_K = 5
CONV_CH = 2 * GD_KEY + GD_WIDTH
N_EXPERTS = 8
TOP_K = 2
D_FF_DENSE = 11008
D_FF_EXPERT = 4096
DEEPNORM_ALPHA = (2 * DEPTH) ** 0.25
DEEPNORM_BETA = (8 * DEPTH) ** -0.25
LN_EPS = 1e-5
RMS_EPS = 1e-6

IN_SPLITS = (
    ('hg_q', HG_KEY), ('hg_f_fwd', HG_KEY), ('hg_f_bwd', HG_KEY), ('hg_i', HG_WIDTH), ('hg_g', HG_WIDTH),
    ('gd_q', GD_KEY), ('gd_k', GD_KEY), ('gd_v', GD_WIDTH),
    ('gd_a_fwd', GD_HEADS), ('gd_a_bwd', GD_HEADS), ('gd_b_fwd', GD_HEADS), ('gd_b_bwd', GD_HEADS),
    ('gd_g', GD_WIDTH),
)
IN_COLS = sum(n for _, n in IN_SPLITS)

kernel_name = "hgrn2_gdn_hybrid_deepnorm_moe_encoder"


def layer_norm(x, g, b):
    xf = x.astype(jnp.float32)
    mu = jnp.mean(xf, axis=-1, keepdims=True)
    xc = xf - mu
    var = jnp.mean(xc * xc, axis=-1, keepdims=True)
    return (xc * lax.rsqrt(var + LN_EPS) * g.astype(jnp.float32) + b.astype(jnp.float32)).astype(x.dtype)


def rms_norm_heads(o, g):
    return o * lax.rsqrt(jnp.mean(o * o, axis=-1, keepdims=True) + RMS_EPS) * g.astype(jnp.float32)


def l2_normalize(t):
    return t * lax.rsqrt(jnp.sum(t * t, axis=-1, keepdims=True) + RMS_EPS)


def split_columns(h):
    offs = np.cumsum([n for _, n in IN_SPLITS])[:-1].tolist()
    parts = jnp.split(h, offs, axis=-1)
    return dict(zip([name for name, _ in IN_SPLITS], parts))


def to_heads(t, n_heads):
    b, s, _ = t.shape
    return t.reshape(b, s, n_heads, -1).transpose(0, 2, 1, 3)


def from_heads(t):
    b, h, s, d = t.shape
    return t.transpose(0, 2, 1, 3).reshape(b, s, h * d)


def flip_seq(t):
    return jnp.flip(t, axis=2)


def bidirectional(fn, fwd_args, bwd_args):
    return fn(*fwd_args) + flip_seq(fn(*[flip_seq(a) for a in bwd_args]))


def depthwise_conv(x, w):
    c = x.shape[-1]
    return lax.conv_general_dilated(x, w[:, None, :].astype(x.dtype), window_strides=(1,),
                                    padding=[(CONV_K // 2, CONV_K // 2)],
                                    dimension_numbers=('NWC', 'WIO', 'NWC'),
                                    feature_group_count=c)


def hgrn2_chunk_scan(q, k, v, log_f):
    b, h, s, dk = q.shape
    dv = v.shape[-1]
    n = s // CHUNK

    def chunks(t):
        return jnp.moveaxis(t.reshape(b, h, n, CHUNK, t.shape[-1]), 2, 0)

    incl = jnp.tril(jnp.ones((CHUNK, CHUNK), bool))

    def step(state, inp):
        qc, kc, vc, gc = inp
        cum = jnp.cumsum(gc, axis=2)
        diff = cum[:, :, :, None, :] - cum[:, :, None, :, :]
        decay = jnp.exp(jnp.where(incl[:, :, None], diff, -jnp.inf))
        attn = jnp.einsum('bhtd,bhsd,bhtsd->bhts', qc, kc, decay)
        o = (jnp.einsum('bhts,bhse->bhte', attn, vc)
             + jnp.einsum('bhtd,bhde->bhte', qc * jnp.exp(cum), state))
        last = cum[:, :, -1:, :]
        state = (jnp.exp(last[:, :, 0, :])[..., None] * state
                 + jnp.einsum('bhsd,bhse->bhde', kc * jnp.exp(last - cum), vc))
        return state, o

    s0 = jnp.zeros((b, h, dk, dv), jnp.float32)
    _, o = lax.scan(step, s0, (chunks(q), chunks(k), chunks(v), chunks(log_f)))
    return jnp.moveaxis(o, 0, 2).reshape(b, h, s, dv)


def gated_delta_chunk_scan(q, k, v, beta, log_a):
    b, h, s, dk = q.shape
    dv = v.shape[-1]
    n = s // CHUNK

    def ch(t):
        return t.reshape(b, h, n, CHUNK, *t.shape[3:])

    q, k, v, beta, log_a = ch(q), ch(k), ch(v), ch(beta), ch(log_a)
    cum = jnp.cumsum(log_a, axis=-1)
    incl = jnp.tril(jnp.ones((CHUNK, CHUNK), bool))
    strict = jnp.tril(jnp.ones((CHUNK, CHUNK), bool), k=-1)
    decay = jnp.exp(jnp.where(incl, cum[..., :, None] - cum[..., None, :], -jnp.inf))
    kb = k * beta[..., None]
    a_mat = jnp.where(strict, jnp.einsum('bhntd,bhnsd->bhnts', kb, k) * decay, 0.0)
    eye = jnp.eye(CHUNK, dtype=jnp.float32)
    t_mat = lax.linalg.triangular_solve(eye + a_mat, jnp.broadcast_to(eye, a_mat.shape),
                                        left_side=True, lower=True, unit_diagonal=True)
    u = jnp.einsum('bhnts,bhnse->bhnte', t_mat, v * beta[..., None])
    w = jnp.einsum('bhnts,bhnsd->bhntd', t_mat, kb * jnp.exp(cum)[..., None])
    qk = jnp.einsum('bhntd,bhnsd->bhnts', q, k) * decay
    q_dec = q * jnp.exp(cum)[..., None]
    k_dec = k * jnp.exp(cum[..., -1:] - cum)[..., None]
    chunk_decay = jnp.exp(cum[..., -1])

    def step(state, inp):
        u_c, w_c, qk_c, qd_c, kd_c, cd_c = inp
        v_new = u_c - jnp.einsum('bhtd,bhde->bhte', w_c, state)
        o = jnp.einsum('bhtd,bhde->bhte', qd_c, state) + jnp.einsum('bhts,bhse->bhte', qk_c, v_new)
        state = cd_c[..., None, None] * state + jnp.einsum('bhsd,bhse->bhde', kd_c, v_new)
        return state, o

    xs = tuple(jnp.moveaxis(t, 2, 0) for t in (u, w, qk, q_dec, k_dec, chunk_decay))
    _, o = lax.scan(step, jnp.zeros((b, h, dk, dv), jnp.float32), xs)
    return jnp.moveaxis(o, 0, 2).reshape(b, h, s, dv)


def hybrid_mixer(x, w_in, lb, conv_w, a_log, dt_bias, hg_norm_g, gd_norm_g, w_out):
    f32 = jnp.float32
    p = split_columns(jnp.einsum('bsd,dc->bsc', x, w_in))

    hq = to_heads(p['hg_q'], HG_HEADS).astype(f32) * (HG_DK ** -0.5)
    hv = to_heads(p['hg_i'], HG_HEADS).astype(f32)

    def hg_forget(z, lb_dir):
        lbh = lb_dir.reshape(HG_HEADS, 1, HG_DK)
        log_f = jnp.logaddexp(jnp.log(lbh),
                              jnp.log1p(-lbh) + jax.nn.log_sigmoid(to_heads(z, HG_HEADS).astype(f32)))
        return -jnp.expm1(log_f), log_f

    k_f, lf_f = hg_forget(p['hg_f_fwd'], lb[0])
    k_b, lf_b = hg_forget(p['hg_f_bwd'], lb[1])
    o_hg = bidirectional(hgrn2_chunk_scan, (hq, k_f, hv, lf_f), (hq, k_b, hv, lf_b))
    hg_out = from_heads(rms_norm_heads(o_hg, hg_norm_g)) * jax.nn.silu(p['hg_g'].astype(f32))

    qkv = jnp.concatenate([p['gd_q'], p['gd_k'], p['gd_v']], axis=-1)
    qkv = jax.nn.silu(depthwise_conv(qkv, conv_w))
    gq, gk, gv = jnp.split(qkv, [GD_KEY, 2 * GD_KEY], axis=-1)
    gq = l2_normalize(to_heads(gq, GD_HEADS).astype(f32)) * (GD_DK ** -0.5)
    gk = l2_normalize(to_heads(gk, GD_HEADS).astype(f32))
    gv = to_heads(gv, GD_HEADS).astype(f32)

    def gd_gates(a, bl, d):
        beta = jax.nn.sigmoid(bl.astype(f32)).transpose(0, 2, 1)
        log_alpha = (-jnp.exp(a_log[d].astype(f32))
                     * jax.nn.softplus(a.astype(f32) + dt_bias[d].astype(f32))).transpose(0, 2, 1)
        return beta, log_alpha

    beta_f, la_f = gd_gates(p['gd_a_fwd'], p['gd_b_fwd'], 0)
    beta_b, la_b = gd_gates(p['gd_a_bwd'], p['gd_b_bwd'], 1)
    o_gd = bidirectional(gated_delta_chunk_scan, (gq, gk, gv, beta_f, la_f), (gq, gk, gv, beta_b, la_b))
    gd_out = from_heads(rms_norm_heads(o_gd, gd_norm_g)) * jax.nn.silu(p['gd_g'].astype(f32))

    merged = jnp.concatenate([hg_out, gd_out], axis=-1).astype(x.dtype)
    return jnp.einsum('bsc,cd->bsd', merged, w_out)


def swiglu(x, w_gate, w_up, w_down):
    hdn = jax.nn.silu(jnp.einsum('...d,df->...f', x, w_gate)) * jnp.einsum('...d,df->...f', x, w_up)
    return jnp.einsum('...f,fd->...d', hdn, w_down)


def moe_swiglu(x, router, w_gate, w_up, w_down):
    b, s, d = x.shape
    xt = x.reshape(b * s, d)
    logits = jnp.einsum('td,de->te', xt.astype(jnp.float32), router.astype(jnp.float32))
    top_val, top_idx = lax.top_k(logits, TOP_K)
    top_w = jax.nn.softmax(top_val, axis=-1)
    gates = jnp.einsum('tk,tke->te', top_w, jax.nn.one_hot(top_idx, N_EXPERTS, dtype=jnp.float32))
    y = jnp.zeros((b * s, d), jnp.float32)
    for e in range(N_EXPERTS):
        y = y + gates[:, e:e + 1] * swiglu(xt, w_gate[e], w_up[e], w_down[e]).astype(jnp.float32)
    return y.reshape(b, s, d).astype(x.dtype)


def setup_inputs(seed: int = 0) -> dict:
    key = jax.random.key(seed)
    ks = jax.random.split(key, 24)
    f32 = jnp.float32
    n_dense = (DEPTH + 1) // 2
    n_moe = DEPTH // 2
    nrm = lambda k, shape, scale: jax.random.normal(k, shape, f32) * scale
    dt = jnp.exp(jax.random.uniform(ks[6], (DEPTH, 2, GD_HEADS), f32, np.log(1e-3), np.log(1e-1)))
    return {
        'x': nrm(ks[0], (BATCH, SEQ, D_MODEL), 1.0),
        'ln_in_g': 1.0 + nrm(ks[1], (D_MODEL,), 0.02),
        'ln_in_b': nrm(ks[2], (D_MODEL,), 0.02),
        'w_in': nrm(ks[3], (DEPTH, D_MODEL, IN_COLS), D_MODEL ** -0.5),
        'hg_lb_param': nrm(ks[4], (DEPTH, 2, HG_KEY), 0.5),
        'gd_conv_w': nrm(ks[5], (DEPTH, CONV_K, CONV_CH), CONV_K ** -0.5),
        'gd_a_log': jnp.log(jax.random.uniform(ks[7], (DEPTH, 2, GD_HEADS), f32, 1.0, 16.0)),
        'gd_dt_bias': dt + jnp.log(-jnp.expm1(-dt)),
        'hg_norm_g': 1.0 + nrm(ks[8], (DEPTH, HG_DV), 0.02),
        'gd_norm_g': 1.0 + nrm(ks[9], (DEPTH, GD_DV), 0.02),
        'w_out': nrm(ks[10], (DEPTH, MIX_WIDTH, D_MODEL), DEEPNORM_BETA * MIX_WIDTH ** -0.5),
        'ln1_g': 1.0 + nrm(ks[11], (DEPTH, D_MODEL), 0.02),
        'ln1_b': nrm(ks[12], (DEPTH, D_MODEL), 0.02),
        'ffn_w_gate': nrm(ks[13], (n_dense, D_MODEL, D_FF_DENSE), D_MODEL ** -0.5),
        'ffn_w_up': nrm(ks[14], (n_dense, D_MODEL, D_FF_DENSE), D_MODEL ** -0.5),
        'ffn_w_down': nrm(ks[15], (n_dense, D_FF_DENSE, D_MODEL), DEEPNORM_BETA * D_FF_DENSE ** -0.5),
        'moe_router': nrm(ks[16], (n_moe, D_MODEL, N_EXPERTS), D_MODEL ** -0.5),
        'moe_w_gate': nrm(ks[17], (n_moe, N_EXPERTS, D_MODEL, D_FF_EXPERT), D_MODEL ** -0.5),
        'moe_w_up': nrm(ks[18], (n_moe, N_EXPERTS, D_MODEL, D_FF_EXPERT), D_MODEL ** -0.5),
        'moe_w_down': nrm(ks[19], (n_moe, N_EXPERTS, D_FF_EXPERT, D_MODEL), DEEPNORM_BETA * D_FF_EXPERT ** -0.5),
        'ln2_g': 1.0 + nrm(ks[20], (DEPTH, D_MODEL), 0.02),
        'ln2_b': nrm(ks[21], (DEPTH, D_MODEL), 0.02),
    }


def reference(x, ln_in_g, ln_in_b, w_in, hg_lb_param, gd_conv_w, gd_a_log, gd_dt_bias,
              hg_norm_g, gd_norm_g, w_out, ln1_g, ln1_b, ffn_w_gate, ffn_w_up, ffn_w_down,
              moe_router, moe_w_gate, moe_w_up, moe_w_down, ln2_g, ln2_b):
    lb = jnp.cumsum(jax.nn.softmax(hg_lb_param.astype(jnp.float32), axis=0), axis=0)
    lb = lb - lb[0:1]
    x = layer_norm(x, ln_in_g, ln_in_b)
    for l in range(DEPTH):
        mix = hybrid_mixer(x, w_in[l], lb[l], gd_conv_w[l], gd_a_log[l], gd_dt_bias[l],
                           hg_norm_g[l], gd_norm_g[l], w_out[l])
        x = layer_norm(DEEPNORM_ALPHA * x + mix, ln1_g[l], ln1_b[l])
        j = l // 2
        if l % 2 == 0:
            ffn = swiglu(x, ffn_w_gate[j], ffn_w_up[j], ffn_w_down[j])
        else:
            ffn = moe_swiglu(x, moe_router[j], moe_w_gate[j], moe_w_up[j], moe_w_down[j])
        x = layer_norm(DEEPNORM_ALPHA * x + ffn, ln2_g[l], ln2_b[l])
    return x
```

```python
import functools
import math

import numpy as np
import jax
import jax.numpy as jnp
from jax import lax
from jax.experimental import pallas as pl
from jax.experimental.pallas import tpu as pltpu

F32 = jnp.float32
BF16 = jnp.bfloat16

LANE = 128
HEAD_DIM = 128
CHUNK = 64
CONV_K = 5
N_GATE_COLS = 4
LN_EPS = 1e-5
RMS_EPS = 1e-6
VMEM_LIMIT_BYTES = 56 * 1024 * 1024


def _cparams(sem):
    return pltpu.CompilerParams(dimension_semantics=sem, vmem_limit_bytes=VMEM_LIMIT_BYTES)


def _pick(n, pref):
    t = min(n, pref)
    while n % t:
        t -= LANE
    return t


def _mm_kernel(a_ref, b_ref, o_ref):
    o_ref[...] = jnp.dot(a_ref[...], b_ref[...], preferred_element_type=F32).astype(o_ref.dtype)


def _mm_acc_kernel(a_ref, b_ref, o_ref, acc_ref, *, nk):
    k = pl.program_id(2)
    prod = jnp.dot(a_ref[...], b_ref[...], preferred_element_type=F32)

    @pl.when(k == 0)
    def _():
        acc_ref[...] = prod

    @pl.when(k > 0)
    def _():
        acc_ref[...] += prod

    @pl.when(k == nk - 1)
    def _():
        o_ref[...] = acc_ref[...].astype(o_ref.dtype)


def _matmul(a, b, out_dtype, tm=1024, tn=512, tk=None, name="matmul"):
    m, kdim = a.shape
    n = b.shape[1]
    tm, tn = _pick(m, tm), _pick(n, tn)
    tk = kdim if tk is None else _pick(kdim, tk)
    nk = kdim // tk
    if nk == 1:
        return pl.pallas_call(
            _mm_kernel,
            grid=(m // tm, n // tn),
            in_specs=[pl.BlockSpec((tm, kdim), lambda i, j: (i, 0)),
                      pl.BlockSpec((kdim, tn), lambda i, j: (0, j))],
            out_specs=pl.BlockSpec((tm, tn), lambda i, j: (i, j)),
            out_shape=jax.ShapeDtypeStruct((m, n), out_dtype),
            compiler_params=_cparams(("parallel", "arbitrary")),
            name=name,
        )(a, b)
    return pl.pallas_call(
        functools.partial(_mm_acc_kernel, nk=nk),
        grid=(m // tm, n // tn, nk),
        in_specs=[pl.BlockSpec((tm, tk), lambda i, j, k: (i, k)),
                  pl.BlockSpec((tk, tn), lambda i, j, k: (k, j))],
        out_specs=pl.BlockSpec((tm, tn), lambda i, j, k: (i, j)),
        out_shape=jax.ShapeDtypeStruct((m, n), out_dtype),
        scratch_shapes=[pltpu.VMEM((tm, tn), F32)],
        compiler_params=_cparams(("parallel", "arbitrary", "arbitrary")),
        name=name,
    )(a, b)


def _sigmoid(x):
    return 1.0 / (1.0 + jnp.exp(-x))


def _swiglu_kernel(a_ref, wg_ref, wu_ref, *rest, gated):
    if gated:
        gate_ref, o_ref = rest
    else:
        (o_ref,) = rest
    a = a_ref[...]
    g = jnp.dot(a, wg_ref[...], preferred_element_type=F32)
    u = jnp.dot(a, wu_ref[...], preferred_element_type=F32)
    h = g * _sigmoid(g) * u
    if gated:
        e = pl.program_id(1)
        gates = gate_ref[...]
        lane = lax.broadcasted_iota(jnp.int32, gates.shape, 1)
        h = h * jnp.sum(jnp.where(lane == e, gates, 0.0), axis=1, keepdims=True)
    o_ref[...] = h.astype(o_ref.dtype)


def _swiglu(a, wg, wu, gates=None, tm=1024, tn=512, name="swiglu"):
    m, kdim = a.shape
    ne, _, f = wg.shape
    tm, tn = _pick(m, tm), _pick(f, tn)
    nf = f // tn
    in_specs = [pl.BlockSpec((tm, kdim), lambda i, e, j: (i, 0)),
                pl.BlockSpec((None, kdim, tn), lambda i, e, j: (e, 0, j)),
                pl.BlockSpec((None, kdim, tn), lambda i, e, j: (e, 0, j))]
    args = [a, wg, wu]
    if gates is not None:
        in_specs.append(pl.BlockSpec((tm, LANE), lambda i, e, j: (i, 0)))
        args.append(gates)
    return pl.pallas_call(
        functools.partial(_swiglu_kernel, gated=gates is not None),
        grid=(m // tm, ne, nf),
        in_specs=in_specs,
        out_specs=pl.BlockSpec((tm, tn), lambda i, e, j: (i, e * nf + j)),
        out_shape=jax.ShapeDtypeStruct((m, ne * f), BF16),
        compiler_params=_cparams(("parallel", "arbitrary", "arbitrary")),
        name=name,
    )(*args)


def _ln_kernel(*refs, alpha, has_res):
    if has_res:
        x_ref, r_ref, g_ref, b_ref, of_ref, ob_ref = refs
        x = alpha * x_ref[...] + r_ref[...]
    else:
        x_ref, g_ref, b_ref, of_ref, ob_ref = refs
        x = x_ref[...]
    mu = jnp.mean(x, axis=-1, keepdims=True)
    xc = x - mu
    var = jnp.mean(xc * xc, axis=-1, keepdims=True)
    y = xc * lax.rsqrt(var + LN_EPS) * g_ref[...] + b_ref[...]
    of_ref[...] = y
    ob_ref[...] = y.astype(BF16)


def _layer_norm(x, res, g, b, alpha=1.0, tm=256):
    m, d = x.shape
    tm = _pick(m, tm)
    row = pl.BlockSpec((tm, d), lambda i: (i, 0))
    vec = pl.BlockSpec((1, d), lambda i: (0, 0))
    args = [x] + ([res] if res is not None else []) + [g.reshape(1, d), b.reshape(1, d)]
    return pl.pallas_call(
        functools.partial(_ln_kernel, alpha=alpha, has_res=res is not None),
        grid=(m // tm,),
        in_specs=[row] * (len(args) - 2) + [vec, vec],
        out_specs=[row, row],
        out_shape=[jax.ShapeDtypeStruct((m, d), F32), jax.ShapeDtypeStruct((m, d), BF16)],
        compiler_params=_cparams(("parallel",)),
        name="layer_norm",
    )(*args)


def _router_kernel(x_ref, r_ref, o_ref, *, n_experts):
    logits = jnp.dot(x_ref[...], r_ref[...], precision=lax.Precision.HIGHEST, preferred_element_type=F32)
    lane = lax.broadcasted_iota(jnp.int32, logits.shape, 1).astype(F32)
    neg = -jnp.inf
    lg = jnp.where(lane < n_experts, logits, neg)
    m1 = jnp.max(lg, axis=1, keepdims=True)
    i1 = jnp.min(jnp.where(lg == m1, lane, float(LANE)), axis=1, keepdims=True)
    lg2 = jnp.where(lane == i1, neg, lg)
    m2 = jnp.max(lg2, axis=1, keepdims=True)
    i2 = jnp.min(jnp.where(lg2 == m2, lane, float(LANE)), axis=1, keepdims=True)
    e2 = jnp.exp(m2 - m1)
    w1 = 1.0 / (1.0 + e2)
    o_ref[...] = jnp.where(lane == i1, w1, 0.0) + jnp.where(lane == i2, e2 * w1, 0.0)


def _router_gates(x, router, tm=512):
    m, d = x.shape
    ne = router.shape[1]
    tm = _pick(m, tm)
    rpad = jnp.zeros((d, LANE), F32).at[:, :ne].set(router.astype(F32))
    return pl.pallas_call(
        functools.partial(_router_kernel, n_experts=ne),
        grid=(m // tm,),
        in_specs=[pl.BlockSpec((tm, d), lambda i: (i, 0)), pl.BlockSpec((d, LANE), lambda i: (0, 0))],
        out_specs=pl.BlockSpec((tm, LANE), lambda i: (i, 0)),
        out_shape=jax.ShapeDtypeStruct((m, LANE), F32),
        compiler_params=_cparams(("parallel",)),
        name="router",
    )(x, rpad)


def _dot(a, b):
    return jnp.dot(a, b, preferred_element_type=F32)


def _dot_nt(a, b):
    return lax.dot_general(a, b, (((1,), (1,)), ((), ())), preferred_element_type=F32)


def _dot_tn(a, b):
    return lax.dot_general(a, b, (((0,), (0,)), ((), ())), preferred_element_type=F32)


def _split3(x):
    hi = x.astype(BF16)
    r1 = x - hi.astype(F32)
    mid = r1.astype(BF16)
    lo = (r1 - mid.astype(F32)).astype(BF16)
    return hi, mid, lo


def _dot_exact_lhs(stat, x):
    hi, mid, lo = _split3(x)
    return _dot(stat, hi) + _dot(stat, mid) + _dot(stat, lo)


def _dot_exact_rhs(x, stat):
    hi, mid, lo = _split3(x)
    return _dot(hi, stat) + _dot(mid, stat) + _dot(lo, stat)


def _dot_hi(a, b):
    a_hi = a.astype(BF16)
    a_lo = (a - a_hi.astype(F32)).astype(BF16)
    b_hi = b.astype(BF16)
    b_lo = (b - b_hi.astype(F32)).astype(BF16)
    return _dot(a_hi, b_hi) + _dot(a_hi, b_lo) + _dot(a_lo, b_hi)


def _log_sigmoid(z):
    return jnp.minimum(z, 0.0) - jnp.log(1.0 + jnp.exp(-jnp.abs(z)))


def _softplus(z):
    return jnp.maximum(z, 0.0) + jnp.log(1.0 + jnp.exp(-jnp.abs(z)))


def _n_levels(c):
    return int(round(math.log2(c)))


@functools.lru_cache(maxsize=None)
def _hg_static(c, rev):
    n = _n_levels(c)
    t = np.arange(c)[:, None]
    r = np.arange(c)[None, :]
    mats = [r <= t]
    d_mats, e_mats, masks = [], [], [t == r]
    for lvl in range(1, n + 1):
        blk, half = 1 << lvl, 1 << (lvl - 1)
        b0 = (t // blk) * blk
        mid = b0 + half - 1
        second = (t - b0) >= half
        d_mats.append(second & (r > mid) & (r <= t))
        e_mats.append((~second) & (r > t) & (r <= mid))
        masks.append((b0 == (r // blk) * blk) & second & ((r % blk) < half))
    mats = mats + d_mats + e_mats + [r > t]
    if rev:
        mats = [m[::-1, ::-1] for m in mats]
        masks = [m[::-1, ::-1] for m in masks]
    stat = np.concatenate(mats, axis=0).astype(np.float32)
    return stat, np.stack(masks).astype(np.float32)


def _hgrn2_kernel(*refs, rev, nchunk, epilogue, scale):
    if epilogue:
        (q_ref, z_ref, v_ref, llb_ref, l1m_ref, stat_ref, mask_ref,
         of_ref, og_ref, ng_ref, o_ref, st_ref) = refs
    else:
        q_ref, z_ref, v_ref, llb_ref, l1m_ref, stat_ref, mask_ref, o_ref, st_ref = refs
    c = CHUNK
    n = _n_levels(c)

    @pl.when(pl.program_id(2) == 0)
    def _():
        st_ref[...] = jnp.zeros_like(st_ref)

    stat = stat_ref[...]
    log_lb = llb_ref[...]
    log_1m_lb = l1m_ref[...]

    def chunk_step(ci, carry):
        cc = (nchunk - 1 - ci) if rev else ci
        rows = pl.ds(pl.multiple_of(cc * c, c), c)
        q = q_ref[rows, :] * scale
        v = v_ref[rows, :]
        b = log_1m_lb + _log_sigmoid(z_ref[rows, :])
        g = jnp.maximum(log_lb, b) + jnp.log(1.0 + jnp.exp(-jnp.abs(log_lb - b)))
        k = 1.0 - jnp.exp(g)
        x = _dot_exact_lhs(stat, g)
        cum = x[0:c]
        att = mask_ref[0] * _dot_nt(q.astype(BF16), k.astype(BF16))
        for lvl in range(1, n + 1):
            ql = (q * jnp.exp(x[lvl * c:(lvl + 1) * c])).astype(BF16)
            kl = (k * jnp.exp(x[(n + lvl) * c:(n + lvl + 1) * c])).astype(BF16)
            att = att + mask_ref[lvl] * _dot_nt(ql, kl)
        st = st_ref[...]
        vb = v.astype(BF16)
        o = _dot(att.astype(BF16), vb) + _dot_nt((q * jnp.exp(cum)).astype(BF16), st.astype(BF16))
        kd = (k * jnp.exp(x[(2 * n + 1) * c:(2 * n + 2) * c])).astype(BF16)
        edge = cum[0:1] if rev else cum[c - 1:c]
        st_ref[...] = st * jnp.exp(edge) + _dot_tn(vb, kd)
        if epilogue:
            tot = o + of_ref[rows, :]
            y = tot * lax.rsqrt(jnp.mean(tot * tot, axis=-1, keepdims=True) + RMS_EPS) * ng_ref[...]
            gate = og_ref[rows, :]
            o_ref[rows, :] = (y * gate * _sigmoid(gate)).astype(o_ref.dtype)
        else:
            o_ref[rows, :] = o
        return carry

    lax.fori_loop(0, nchunk, chunk_step, 0)


def _hgrn2(p3, o_fwd, llb, l1m, norm_g, *, rev, heads, col, d_model, rows=512):
    bsz, s, _ = p3.shape
    rows = _pick(s, rows)
    ns = s // rows
    stat, masks = _hg_static(CHUNK, rev)
    stat = jnp.asarray(stat, BF16)
    masks = jnp.asarray(masks, F32)
    zcol = col["hg_f_bwd"] if rev else col["hg_f_fwd"]
    seq = (lambda i: ns - 1 - i) if rev else (lambda i: i)

    def pspec(c0):
        return pl.BlockSpec((None, rows, LANE), lambda b, h, i: (b, seq(i), c0 + h))

    vec = pl.BlockSpec((None, 1, LANE), lambda b, h, i: (h, 0, 0))
    in_specs = [pspec(col["hg_q"]), pspec(zcol), pspec(col["hg_i"]), vec, vec,
                pl.BlockSpec(stat.shape, lambda b, h, i: (0, 0)),
                pl.BlockSpec(masks.shape, lambda b, h, i: (0, 0, 0))]
    args = [p3, p3, p3, llb, l1m, stat, masks]
    if rev:
        in_specs += [pl.BlockSpec((None, rows, LANE), lambda b, h, i: (b, seq(i), h)),
                     pspec(col["hg_g"]),
                     pl.BlockSpec((1, LANE), lambda b, h, i: (0, 0))]
        args += [o_fwd, p3, norm_g]
        out_shape = jax.ShapeDtypeStruct((bsz, s, d_model), BF16)
    else:
        out_shape = jax.ShapeDtypeStruct((bsz, s, heads * LANE), F32)
    return pl.pallas_call(
        functools.partial(_hgrn2_kernel, rev=rev, nchunk=rows // CHUNK, epilogue=rev, scale=HEAD_DIM ** -0.5),
        grid=(bsz, heads, ns),
        in_specs=in_specs,
        out_specs=pl.BlockSpec((None, rows, LANE), lambda b, h, i: (b, seq(i), h)),
        out_shape=out_shape,
        scratch_shapes=[pltpu.VMEM((HEAD_DIM, HEAD_DIM), F32)],
        compiler_params=_cparams(("parallel", "parallel", "arbitrary")),
        name="hgrn2_bwd" if rev else "hgrn2_fwd",
    )(*args)


def _gd_prep_kernel(x_ref, w_ref, o_ref, xp_ref, *, s, tile, n_qk, n_q, scale_q):
    j = pl.program_id(1)
    halo = 8
    xp_ref[0:halo, :] = jnp.zeros((halo, LANE), F32)
    xp_ref[halo + s:2 * halo + s, :] = jnp.zeros((halo, LANE), F32)
    xp_ref[halo:halo + s, :] = x_ref[...]
    w = w_ref[...]
    inv_scale = jnp.where(j < n_q, scale_q, 1.0)

    def tile_step(ti, carry):
        r0 = pl.multiple_of(ti * tile, tile)
        win = xp_ref[pl.ds(r0, tile + 2 * halo), :]
        acc = jnp.zeros((tile, LANE), F32)
        for tap in range(CONV_K):
            off = halo + tap - CONV_K // 2
            acc = acc + w[tap:tap + 1, :] * win[off:off + tile, :]
        y = acc * _sigmoid(acc)
        inv = lax.rsqrt(jnp.sum(y * y, axis=-1, keepdims=True) + RMS_EPS) * inv_scale
        o_ref[pl.ds(r0, tile), :] = y * jnp.where(j < n_qk, inv, 1.0)
        return carry

    lax.fori_loop(0, s // tile, tile_step, 0)


def _gd_prep(p3, conv_w, *, col0, heads, tile=256):
    bsz, s, _ = p3.shape
    tile = _pick(s, tile)
    nblk = 3 * heads
    return pl.pallas_call(
        functools.partial(_gd_prep_kernel, s=s, tile=tile, n_qk=2 * heads, n_q=heads, scale_q=HEAD_DIM ** -0.5),
        grid=(bsz, nblk),
        in_specs=[pl.BlockSpec((None, s, LANE), lambda b, j: (b, 0, col0 + j)),
                  pl.BlockSpec((CONV_K, LANE), lambda b, j: (0, j))],
        out_specs=pl.BlockSpec((None, s, LANE), lambda b, j: (b, 0, j)),
        out_shape=jax.ShapeDtypeStruct((bsz, s, nblk * LANE), F32),
        scratch_shapes=[pltpu.VMEM((s + 16, LANE), F32)],
        compiler_params=_cparams(("parallel", "parallel")),
        name="gd_prep",
    )(p3, conv_w)


@functools.lru_cache(maxsize=None)
def _gd_static(c, rev):
    t = np.arange(c)[:, None]
    s = np.arange(c)[None, :]
    incl = (s >= t) if rev else (s <= t)
    strict = (s > t) if rev else (s < t)
    mats = [incl, strict, t == s, np.ones((c, c), bool), strict & ((t // 8) == (s // 8))]
    blk = 16
    while blk <= c:
        half = blk // 2
        same = (t // blk) == (s // blk)
        t2, s2 = (t % blk) >= half, (s % blk) >= half
        mats.append(same & ((~t2) & s2 if rev else t2 & (~s2)))
        blk *= 2
    return np.stack(mats).astype(np.float32)


def _gdn_kernel(*refs, rev, nchunk, epilogue, heads):
    if epilogue:
        (q_ref, k_ref, v_ref, pg_ref, na_ref, dtb_ref, stat_ref,
         of_ref, og_ref, ng_ref, _merged_in, o_ref, s_ref) = refs
    else:
        q_ref, k_ref, v_ref, pg_ref, na_ref, dtb_ref, stat_ref, o_ref, s_ref = refs
    c = CHUNK
    h = pl.program_id(1)

    @pl.when(pl.program_id(2) == 0)
    def _():
        s_ref[...] = jnp.zeros_like(s_ref)

    incl = stat_ref[0]
    strict = stat_ref[1]
    eye = stat_ref[2]
    incl_b = incl.astype(BF16)
    ones_b = stat_ref[3].astype(BF16)
    n_lvl = stat_ref.shape[0] - 5
    rsel = lax.broadcasted_iota(jnp.int32, (LANE, LANE), 0)
    d = 1 if rev else 0
    sel_a = jnp.where(rsel == d * heads + h, 1.0, 0.0).astype(BF16)
    sel_b = jnp.where(rsel == (2 + d) * heads + h, 1.0, 0.0).astype(BF16)
    neg_a = na_ref[...]
    dtb = dtb_ref[...]
    incl_t = 1.0 - strict

    def chunk_step(ci, carry):
        cc = (nchunk - 1 - ci) if rev else ci
        rows = pl.ds(pl.multiple_of(cc * c, c), c)
        q = q_ref[rows, :]
        k = k_ref[rows, :]
        v = v_ref[rows, :]
        pg = pg_ref[rows, :]
        gb = _dot_exact_rhs(neg_a * _softplus(pg + dtb), sel_a)
        bb = _dot_exact_rhs(_sigmoid(pg), sel_b)
        cum = _dot_exact_lhs(incl_b, gb)
        cum_row = _dot_exact_lhs(ones_b, gb[:, 0:c] * incl_t)
        decay = jnp.exp(jnp.where(incl > 0, cum[:, 0:c] - cum_row, -jnp.inf))
        ecum = jnp.exp(cum)
        kb = k * bb
        kbf = k.astype(BF16)
        a = strict * _dot_nt(kb.astype(BF16), kbf) * decay
        a0 = a * stat_ref[4]
        xm = eye - a0
        a2 = _dot_hi(a0, a0)
        ym = xm + _dot_hi(xm, a2)
        a4 = _dot_hi(a2, a2)
        tm = ym + _dot_hi(ym, a4)
        for lvl in range(n_lvl):
            tm = tm - _dot_hi(_dot_hi(tm, a * stat_ref[5 + lvl]), tm)
        tmb = tm.astype(BF16)
        u = _dot(tmb, (v * bb).astype(BF16))
        w = _dot(tmb, (kb * ecum).astype(BF16))
        qk = _dot_nt(q.astype(BF16), kbf) * decay
        edge = cum[0:1] if rev else cum[c - 1:c]
        st = s_ref[...]
        stb = st.astype(BF16)
        v_new = u - _dot(w.astype(BF16), stb)
        vnb = v_new.astype(BF16)
        o = _dot((q * ecum).astype(BF16), stb) + _dot(qk.astype(BF16), vnb)
        s_ref[...] = st * jnp.exp(edge) + _dot_tn((k * jnp.exp(edge - cum)).astype(BF16), vnb)
        if epilogue:
            tot = o + of_ref[rows, :]
            y = tot * lax.rsqrt(jnp.mean(tot * tot, axis=-1, keepdims=True) + RMS_EPS) * ng_ref[...]
            gate = og_ref[rows, :]
            o_ref[rows, :] = (y * gate * _sigmoid(gate)).astype(o_ref.dtype)
        else:
            o_ref[rows, :] = o
        return carry

    lax.fori_loop(0, nchunk, chunk_step, 0)


def _gdn(qkv, pg, p3, o_fwd, merged, neg_a, dtb, norm_g, *, rev, heads, gcol, rows=512):
    bsz, s, _ = qkv.shape
    rows = _pick(s, rows)
    ns = s // rows
    stat = jnp.asarray(_gd_static(CHUNK, rev), F32)
    seq = (lambda i: ns - 1 - i) if rev else (lambda i: i)

    def qspec(c0):
        return pl.BlockSpec((None, rows, LANE), lambda b, h, i: (b, seq(i), c0 + h))

    vec = pl.BlockSpec((1, LANE), lambda b, h, i: (0, 0))
    in_specs = [qspec(0), qspec(heads), qspec(2 * heads),
                pl.BlockSpec((None, rows, LANE), lambda b, h, i: (b, seq(i), 0)),
                vec, vec, pl.BlockSpec(stat.shape, lambda b, h, i: (0, 0, 0))]
    args = [qkv, qkv, qkv, pg, neg_a, dtb, stat]
    kwargs = {}
    if rev:
        in_specs += [qspec(0), qspec(gcol), vec, pl.BlockSpec(memory_space=pl.ANY)]
        args += [o_fwd, p3, norm_g, merged]
        out_shape = jax.ShapeDtypeStruct(merged.shape, merged.dtype)
        out_spec = pl.BlockSpec((None, rows, LANE), lambda b, h, i: (b, seq(i), heads + h))
        kwargs["input_output_aliases"] = {len(args) - 1: 0}
    else:
        out_shape = jax.ShapeDtypeStruct((bsz, s, heads * LANE), F32)
        out_spec = qspec(0)
    return pl.pallas_call(
        functools.partial(_gdn_kernel, rev=rev, nchunk=rows // CHUNK, epilogue=rev, heads=heads),
        grid=(bsz, heads, ns),
        in_specs=in_specs,
        out_specs=out_spec,
        out_shape=out_shape,
        scratch_shapes=[pltpu.VMEM((HEAD_DIM, HEAD_DIM), F32)],
        compiler_params=_cparams(("parallel", "parallel", "arbitrary")),
        name="gdn_bwd" if rev else "gdn_fwd",
        **kwargs,
    )(*args)


def _pad_cols(w, n):
    return jnp.pad(w, ((0, 0),) * (w.ndim - 1) + ((0, n - w.shape[-1]),))


def kernel(x, ln_in_g, ln_in_b, w_in, hg_lb_param, gd_conv_w, gd_a_log, gd_dt_bias, hg_norm_g, gd_norm_g, w_out,
           ln1_g, ln1_b, ffn_w_gate, ffn_w_up, ffn_w_down, moe_router, moe_w_gate, moe_w_up, moe_w_down,
           ln2_g, ln2_b):
    bsz, s, d = x.shape
    depth = w_in.shape[0]
    m = bsz * s
    half = d // 2
    heads = half // HEAD_DIM
    alpha = (2 * depth) ** 0.25
    n_main = 8 * half
    n_gate = N_GATE_COLS * heads
    names = ("hg_q", "hg_f_fwd", "hg_f_bwd", "hg_i", "hg_g", "gd_q", "gd_k", "gd_v", "gd_g")
    col = {nm: i * heads for i, nm in enumerate(names)}

    lb = jnp.cumsum(jax.nn.softmax(hg_lb_param.astype(F32), axis=0), axis=0)
    lb = lb - lb[0:1]
    log_lb = jnp.log(lb).reshape(depth, 2, heads, 1, LANE)
    log_1m_lb = jnp.log1p(-lb).reshape(depth, 2, heads, 1, LANE)

    xf, xb = _layer_norm(x.reshape(m, d), None, ln_in_g, ln_in_b)
    for l in range(depth):
        w_l = w_in[l]
        w_main = jnp.concatenate([w_l[:, :n_main], w_l[:, n_main + n_gate:]], axis=1).astype(BF16)
        w_gate = _pad_cols(w_l[:, n_main:n_main + n_gate], LANE).astype(BF16)
        p3 = _matmul(xb, w_main, F32, name="in_proj").reshape(bsz, s, -1)
        pg = _matmul(xb, w_gate, F32, tn=LANE, name="gate_proj").reshape(bsz, s, LANE)

        o_f = _hgrn2(p3, None, log_lb[l, 0], log_1m_lb[l, 0], None, rev=False, heads=heads, col=col, d_model=d)
        merged = _hgrn2(p3, o_f, log_lb[l, 1], log_1m_lb[l, 1], hg_norm_g[l].reshape(1, LANE).astype(F32),
                        rev=True, heads=heads, col=col, d_model=d)

        qkv = _gd_prep(p3, gd_conv_w[l].astype(F32), col0=col["gd_q"], heads=heads)
        neg_a = jnp.zeros((1, LANE), F32).at[0, :2 * heads].set(-jnp.exp(gd_a_log[l].astype(F32)).reshape(-1))
        dtb = jnp.zeros((1, LANE), F32).at[0, :2 * heads].set(gd_dt_bias[l].astype(F32).reshape(-1))
        gn = gd_norm_g[l].reshape(1, LANE).astype(F32)
        g_f = _gdn(qkv, pg, None, None, None, neg_a, dtb, None, rev=False, heads=heads, gcol=col["gd_g"])
        merged = _gdn(qkv, pg, p3, g_f, merged, neg_a, dtb, gn, rev=True, heads=heads, gcol=col["gd_g"])

        mix = _matmul(merged.reshape(m, d), w_out[l].astype(BF16), F32, name="out_proj")
        xf, xb = _layer_norm(xf, mix, ln1_g[l], ln1_b[l], alpha=alpha)

        j = l // 2
        if l % 2 == 0:
            f = ffn_w_gate.shape[-1]
            fp = -(-f // 1024) * 1024
            hdn = _swiglu(xb, _pad_cols(ffn_w_gate[j], fp).astype(BF16)[None],
                          _pad_cols(ffn_w_up[j], fp).astype(BF16)[None])
            wd = jnp.pad(ffn_w_down[j], ((0, fp - f), (0, 0))).astype(BF16)
        else:
            gates = _router_gates(xf, moe_router[j])
            hdn = _swiglu(xb, moe_w_gate[j].astype(BF16), moe_w_up[j].astype(BF16), gates=gates)
            wd = moe_w_down[j].astype(BF16).reshape(-1, d)
        ffn = _matmul(hdn, wd, F32, tn=1024, tk=2048 if l % 2 else 1408, name="down_proj")
        xf, xb = _layer_norm(xf, ffn, ln2_g[l], ln2_b[l], alpha=alpha)
    return xf.reshape(bsz, s, d)
```

```python
import functools
import math

import numpy as np
import jax
import jax.numpy as jnp
from jax import lax
from jax.experimental import pallas as pl
from jax.experimental.pallas import tpu as pltpu

F32 = jnp.float32
BF16 = jnp.bfloat16

LANE = 128
HEAD_DIM = 128
CHUNK = 64
CONV_K = 5
N_GATE_COLS = 4
LN_EPS = 1e-5
RMS_EPS = 1e-6
VMEM_LIMIT_BYTES = 56 * 1024 * 1024


def _cparams(sem):
    return pltpu.CompilerParams(dimension_semantics=sem, vmem_limit_bytes=VMEM_LIMIT_BYTES)


def _pick(n, pref):
    t = min(n, pref)
    while n % t:
        t -= LANE
    return t


def _mm_kernel(a_ref, b_ref, o_ref):
    o_ref[...] = jnp.dot(a_ref[...], b_ref[...], preferred_element_type=F32).astype(o_ref.dtype)


def _mm_acc_kernel(a_ref, b_ref, o_ref, acc_ref, *, nk):
    k = pl.program_id(2)
    prod = jnp.dot(a_ref[...], b_ref[...], preferred_element_type=F32)

    @pl.when(k == 0)
    def _():
        acc_ref[...] = prod

    @pl.when(k > 0)
    def _():
        acc_ref[...] += prod

    @pl.when(k == nk - 1)
    def _():
        o_ref[...] = acc_ref[...].astype(o_ref.dtype)


def _matmul(a, b, out_dtype, tm=1024, tn=512, tk=None, name="matmul"):
    m, kdim = a.shape
    n = b.shape[1]
    tm, tn = _pick(m, tm), _pick(n, tn)
    tk = kdim if tk is None else _pick(kdim, tk)
    nk = kdim // tk
    if nk == 1:
        return pl.pallas_call(
            _mm_kernel,
            grid=(m // tm, n // tn),
            in_specs=[pl.BlockSpec((tm, kdim), lambda i, j: (i, 0)),
                      pl.BlockSpec((kdim, tn), lambda i, j: (0, j))],
            out_specs=pl.BlockSpec((tm, tn), lambda i, j: (i, j)),
            out_shape=jax.ShapeDtypeStruct((m, n), out_dtype),
            compiler_params=_cparams(("parallel", "arbitrary")),
            name=name,
        )(a, b)
    return pl.pallas_call(
        functools.partial(_mm_acc_kernel, nk=nk),
        grid=(m // tm, n // tn, nk),
        in_specs=[pl.BlockSpec((tm, tk), lambda i, j, k: (i, k)),
                  pl.BlockSpec((tk, tn), lambda i, j, k: (k, j))],
        out_specs=pl.BlockSpec((tm, tn), lambda i, j, k: (i, j)),
        out_shape=jax.ShapeDtypeStruct((m, n), out_dtype),
        scratch_shapes=[pltpu.VMEM((tm, tn), F32)],
        compiler_params=_cparams(("parallel", "arbitrary", "arbitrary")),
        name=name,
    )(a, b)


def _sigmoid(x):
    return 1.0 / (1.0 + jnp.exp(-x))


def _swiglu_kernel(a_ref, wg_ref, wu_ref, *rest, gated):
    if gated:
        gate_ref, o_ref = rest
    else:
        (o_ref,) = rest
    a = a_ref[...]
    g = jnp.dot(a, wg_ref[...], preferred_element_type=F32)
    u = jnp.dot(a, wu_ref[...], preferred_element_type=F32)
    h = g * _sigmoid(g) * u
    if gated:
        e = pl.program_id(1)
        gates = gate_ref[...]
        lane = lax.broadcasted_iota(jnp.int32, gates.shape, 1)
        h = h * jnp.sum(jnp.where(lane == e, gates, 0.0), axis=1, keepdims=True)
    o_ref[...] = h.astype(o_ref.dtype)


def _swiglu(a, wg, wu, gates=None, tm=1024, tn=512, name="swiglu"):
    m, kdim = a.shape
    ne, _, f = wg.shape
    tm, tn = _pick(m, tm), _pick(f, tn)
    nf = f // tn
    in_specs = [pl.BlockSpec((tm, kdim), lambda i, e, j: (i, 0)),
                pl.BlockSpec((None, kdim, tn), lambda i, e, j: (e, 0, j)),
                pl.BlockSpec((None, kdim, tn), lambda i, e, j: (e, 0, j))]
    args = [a, wg, wu]
    if gates is not None:
        in_specs.append(pl.BlockSpec((tm, LANE), lambda i, e, j: (i, 0)))
        args.append(gates)
    return pl.pallas_call(
        functools.partial(_swiglu_kernel, gated=gates is not None),
        grid=(m // tm, ne, nf),
        in_specs=in_specs,
        out_specs=pl.BlockSpec((tm, tn), lambda i, e, j: (i, e * nf + j)),
        out_shape=jax.ShapeDtypeStruct((m, ne * f), BF16),
        compiler_params=_cparams(("parallel", "arbitrary", "arbitrary")),
        name=name,
    )(*args)


def _ln_kernel(*refs, alpha, has_res):
    if has_res:
        x_ref, r_ref, g_ref, b_ref, of_ref, ob_ref = refs
        x = alpha * x_ref[...] + r_ref[...]
    else:
        x_ref, g_ref, b_ref, of_ref, ob_ref = refs
        x = x_ref[...]
    mu = jnp.mean(x, axis=-1, keepdims=True)
    xc = x - mu
    var = jnp.mean(xc * xc, axis=-1, keepdims=True)
    y = xc * lax.rsqrt(var + LN_EPS) * g_ref[...] + b_ref[...]
    of_ref[...] = y
    ob_ref[...] = y.astype(BF16)


def _layer_norm(x, res, g, b, alpha=1.0, tm=256):
    m, d = x.shape
    tm = _pick(m, tm)
    row = pl.BlockSpec((tm, d), lambda i: (i, 0))
    vec = pl.BlockSpec((1, d), lambda i: (0, 0))
    args = [x] + ([res] if res is not None else []) + [g.reshape(1, d), b.reshape(1, d)]
    return pl.pallas_call(
        functools.partial(_ln_kernel, alpha=alpha, has_res=res is not None),
        grid=(m // tm,),
        in_specs=[row] * (len(args) - 2) + [vec, vec],
        out_specs=[row, row],
        out_shape=[jax.ShapeDtypeStruct((m, d), F32), jax.ShapeDtypeStruct((m, d), BF16)],
        compiler_params=_cparams(("parallel",)),
        name="layer_norm",
    )(*args)


def _router_kernel(x_ref, r_ref, o_ref, *, n_experts):
    logits = jnp.dot(x_ref[...], r_ref[...], precision=lax.Precision.HIGHEST, preferred_element_type=F32)
    lane = lax.broadcasted_iota(jnp.int32, logits.shape, 1).astype(F32)
    neg = -jnp.inf
    lg = jnp.where(lane < n_experts, logits, neg)
    m1 = jnp.max(lg, axis=1, keepdims=True)
    i1 = jnp.min(jnp.where(lg == m1, lane, float(LANE)), axis=1, keepdims=True)
    lg2 = jnp.where(lane == i1, neg, lg)
    m2 = jnp.max(lg2, axis=1, keepdims=True)
    i2 = jnp.min(jnp.where(lg2 == m2, lane, float(LANE)), axis=1, keepdims=True)
    e2 = jnp.exp(m2 - m1)
    w1 = 1.0 / (1.0 + e2)
    o_ref[...] = jnp.where(lane == i1, w1, 0.0) + jnp.where(lane == i2, e2 * w1, 0.0)


def _router_gates(x, router, tm=512):
    m, d = x.shape
    ne = router.shape[1]
    tm = _pick(m, tm)
    rpad = jnp.zeros((d, LANE), F32).at[:, :ne].set(router.astype(F32))
    return pl.pallas_call(
        functools.partial(_router_kernel, n_experts=ne),
        grid=(m // tm,),
        in_specs=[pl.BlockSpec((tm, d), lambda i: (i, 0)), pl.BlockSpec((d, LANE), lambda i: (0, 0))],
        out_specs=pl.BlockSpec((tm, LANE), lambda i: (i, 0)),
        out_shape=jax.ShapeDtypeStruct((m, LANE), F32),
        compiler_params=_cparams(("parallel",)),
        name="router",
    )(x, rpad)


def _dot(a, b):
    return jnp.dot(a, b, preferred_element_type=F32)


def _dot_nt(a, b):
    return lax.dot_general(a, b, (((1,), (1,)), ((), ())), preferred_element_type=F32)


def _dot_tn(a, b):
    return lax.dot_general(a, b, (((0,), (0,)), ((), ())), preferred_element_type=F32)


def _split2(x):
    hi = x.astype(BF16)
    return hi, (x - hi.astype(F32)).astype(BF16)


def _dot_split_lhs(stat, x):
    hi, lo = _split2(x)
    return _dot(stat, hi) + _dot(stat, lo)


def _dot_split_rhs(x, stat):
    hi, lo = _split2(x)
    return _dot(hi, stat) + _dot(lo, stat)


def _dotb(a, b):
    return _dot(a.astype(BF16), b.astype(BF16))


def _log_sigmoid(z):
    return jnp.minimum(z, 0.0) - jnp.log(1.0 + jnp.exp(-jnp.abs(z)))


def _softplus(z):
    return jnp.maximum(z, 0.0) + jnp.log(1.0 + jnp.exp(-jnp.abs(z)))


def _n_levels(c):
    return int(round(math.log2(c)))


def _group_rows(gi, ngroups, unroll, rev):
    gg = (ngroups - 1 - gi) if rev else gi
    order = range(unroll - 1, -1, -1) if rev else range(unroll)
    return [pl.ds(pl.multiple_of((gg * unroll + u) * CHUNK, CHUNK), CHUNK) for u in order]


@functools.lru_cache(maxsize=None)
def _hg_static(c, rev):
    n = _n_levels(c)
    t = np.arange(c)[:, None]
    r = np.arange(c)[None, :]
    mats = [r <= t]
    d_mats, e_mats, masks = [], [], [t == r]
    for lvl in range(1, n + 1):
        blk, half = 1 << lvl, 1 << (lvl - 1)
        b0 = (t // blk) * blk
        mid = b0 + half - 1
        second = (t - b0) >= half
        d_mats.append(second & (r > mid) & (r <= t))
        e_mats.append((~second) & (r > t) & (r <= mid))
        masks.append((b0 == (r // blk) * blk) & second & ((r % blk) < half))
    mats = mats + d_mats + e_mats + [r > t]
    if rev:
        mats = [m[::-1, ::-1] for m in mats]
        masks = [m[::-1, ::-1] for m in masks]
    stat = np.concatenate(mats, axis=0).astype(np.float32)
    return stat, np.stack(masks).astype(np.float32)


def _hgrn2_kernel(*refs, rev, nchunk, unroll, epilogue, scale):
    if epilogue:
        (q_ref, z_ref, v_ref, llb_ref, l1m_ref, stat_ref, mask_ref,
         of_ref, og_ref, ng_ref, o_ref, st_ref) = refs
    else:
        q_ref, z_ref, v_ref, llb_ref, l1m_ref, stat_ref, mask_ref, o_ref, st_ref = refs
    c = CHUNK
    n = _n_levels(c)
    ngroups = nchunk // unroll

    @pl.when(pl.program_id(2) == 0)
    def _():
        st_ref[...] = jnp.zeros_like(st_ref)

    log_lb = llb_ref[...]
    log_1m_lb = l1m_ref[...]

    def group_step(gi, carry):
        rows_list = _group_rows(gi, ngroups, unroll, rev)
        nu = len(rows_list)
        q = [q_ref[rows, :] * scale for rows in rows_list]
        vb = [v_ref[rows, :].astype(BF16) for rows in rows_list]
        b = [log_1m_lb + _log_sigmoid(z_ref[rows, :]) for rows in rows_list]
        g = [jnp.maximum(log_lb, bi) + jnp.log(1.0 + jnp.exp(-jnp.abs(log_lb - bi))) for bi in b]
        k = [1.0 - jnp.exp(gi_) for gi_ in g]
        xall = _dot_split_lhs(stat_ref[...], jnp.concatenate(g, axis=1))

        def expo(blk, i):
            return xall[blk * c:(blk + 1) * c, i * LANE:(i + 1) * LANE]

        att = [mask_ref[0] * _dot_nt(q[i].astype(BF16), k[i].astype(BF16)) for i in range(nu)]
        for lvl in range(1, n + 1):
            ql = [(q[i] * jnp.exp(expo(lvl, i))).astype(BF16) for i in range(nu)]
            kl = [(k[i] * jnp.exp(expo(n + lvl, i))).astype(BF16) for i in range(nu)]
            att = [att[i] + mask_ref[lvl] * _dot_nt(ql[i], kl[i]) for i in range(nu)]
        o_intra = [_dot(att[i].astype(BF16), vb[i]) for i in range(nu)]
        qd = [(q[i] * jnp.exp(expo(0, i))).astype(BF16) for i in range(nu)]
        kd = [(k[i] * jnp.exp(expo(2 * n + 1, i))).astype(BF16) for i in range(nu)]
        dst = [_dot_tn(vb[i], kd[i]) for i in range(nu)]
        edge_row = 0 if rev else c - 1
        de = [jnp.exp(expo(0, i)[edge_row:edge_row + 1]) for i in range(nu)]
        states = [st_ref[...]]
        for i in range(nu):
            states.append(states[i] * de[i] + dst[i])
        st_ref[...] = states[nu]
        outs = [o_intra[i] + _dot_nt(qd[i], states[i].astype(BF16)) for i in range(nu)]
        for rows, o in zip(rows_list, outs):
            if epilogue:
                tot = o + of_ref[rows, :]
                y = tot * lax.rsqrt(jnp.mean(tot * tot, axis=-1, keepdims=True) + RMS_EPS) * ng_ref[...]
                gate = og_ref[rows, :]
                o_ref[rows, :] = (y * gate * _sigmoid(gate)).astype(o_ref.dtype)
            else:
                o_ref[rows, :] = o
        return carry

    lax.fori_loop(0, ngroups, group_step, 0)


def _hgrn2(p3, o_fwd, llb, l1m, norm_g, *, rev, heads, col, d_model, rows=512, unroll=4):
    bsz, s, _ = p3.shape
    rows = _pick(s, rows)
    ns = s // rows
    nchunk = rows // CHUNK
    unroll = math.gcd(unroll, nchunk)
    stat, masks = _hg_static(CHUNK, rev)
    stat = jnp.asarray(stat, BF16)
    masks = jnp.asarray(masks, F32)
    zcol = col["hg_f_bwd"] if rev else col["hg_f_fwd"]
    seq = (lambda i: ns - 1 - i) if rev else (lambda i: i)

    def pspec(c0):
        return pl.BlockSpec((None, rows, LANE), lambda b, h, i: (b, seq(i), c0 + h))

    vec = pl.BlockSpec((None, 1, LANE), lambda b, h, i: (h, 0, 0))
    in_specs = [pspec(col["hg_q"]), pspec(zcol), pspec(col["hg_i"]), vec, vec,
                pl.BlockSpec(stat.shape, lambda b, h, i: (0, 0)),
                pl.BlockSpec(masks.shape, lambda b, h, i: (0, 0, 0))]
    args = [p3, p3, p3, llb, l1m, stat, masks]
    if rev:
        in_specs += [pl.BlockSpec((None, rows, LANE), lambda b, h, i: (b, seq(i), h)),
                     pspec(col["hg_g"]),
                     pl.BlockSpec((1, LANE), lambda b, h, i: (0, 0))]
        args += [o_fwd, p3, norm_g]
        out_shape = jax.ShapeDtypeStruct((bsz, s, d_model), BF16)
    else:
        out_shape = jax.ShapeDtypeStruct((bsz, s, heads * LANE), F32)
    return pl.pallas_call(
        functools.partial(_hgrn2_kernel, rev=rev, nchunk=nchunk, unroll=unroll, epilogue=rev,
                          scale=HEAD_DIM ** -0.5),
        grid=(bsz, heads, ns),
        in_specs=in_specs,
        out_specs=pl.BlockSpec((None, rows, LANE), lambda b, h, i: (b, seq(i), h)),
        out_shape=out_shape,
        scratch_shapes=[pltpu.VMEM((HEAD_DIM, HEAD_DIM), F32)],
        compiler_params=_cparams(("parallel", "parallel", "arbitrary")),
        name="hgrn2_bwd" if rev else "hgrn2_fwd",
    )(*args)


def _gd_prep_kernel(x_ref, w_ref, o_ref, xp_ref, *, s, tile, n_qk, n_q, scale_q):
    j = pl.program_id(1)
    halo = 8
    xp_ref[0:halo, :] = jnp.zeros((halo, LANE), F32)
    xp_ref[halo + s:2 * halo + s, :] = jnp.zeros((halo, LANE), F32)
    xp_ref[halo:halo + s, :] = x_ref[...]
    w = w_ref[...]
    inv_scale = jnp.where(j < n_q, scale_q, 1.0)

    def tile_step(ti, carry):
        r0 = pl.multiple_of(ti * tile, tile)
        win = xp_ref[pl.ds(r0, tile + 2 * halo), :]
        acc = jnp.zeros((tile, LANE), F32)
        for tap in range(CONV_K):
            off = halo + tap - CONV_K // 2
            acc = acc + w[tap:tap + 1, :] * win[off:off + tile, :]
        y = acc * _sigmoid(acc)
        inv = lax.rsqrt(jnp.sum(y * y, axis=-1, keepdims=True) + RMS_EPS) * inv_scale
        o_ref[pl.ds(r0, tile), :] = y * jnp.where(j < n_qk, inv, 1.0)
        return carry

    lax.fori_loop(0, s // tile, tile_step, 0)


def _gd_prep(p3, conv_w, *, col0, heads, tile=256):
    bsz, s, _ = p3.shape
    tile = _pick(s, tile)
    nblk = 3 * heads
    return pl.pallas_call(
        functools.partial(_gd_prep_kernel, s=s, tile=tile, n_qk=2 * heads, n_q=heads, scale_q=HEAD_DIM ** -0.5),
        grid=(bsz, nblk),
        in_specs=[pl.BlockSpec((None, s, LANE), lambda b, j: (b, 0, col0 + j)),
                  pl.BlockSpec((CONV_K, LANE), lambda b, j: (0, j))],
        out_specs=pl.BlockSpec((None, s, LANE), lambda b, j: (b, 0, j)),
        out_shape=jax.ShapeDtypeStruct((bsz, s, nblk * LANE), F32),
        scratch_shapes=[pltpu.VMEM((s + 16, LANE), F32)],
        compiler_params=_cparams(("parallel", "parallel")),
        name="gd_prep",
    )(p3, conv_w)


@functools.lru_cache(maxsize=None)
def _gd_static(c, rev):
    t = np.arange(c)[:, None]
    s = np.arange(c)[None, :]
    incl = (s >= t) if rev else (s <= t)
    strict = (s > t) if rev else (s < t)
    mats = [incl, strict, t == s, np.ones((c, c), bool), strict & ((t // 8) == (s // 8))]
    blk = 16
    while blk <= c:
        half = blk // 2
        same = (t // blk) == (s // blk)
        t2, s2 = (t % blk) >= half, (s % blk) >= half
        mats.append(same & ((~t2) & s2 if rev else t2 & (~s2)))
        blk *= 2
    return np.stack(mats).astype(np.float32)


def _gdn_kernel(*refs, rev, nchunk, unroll, epilogue, heads, hb):
    if epilogue:
        (q_ref, k_ref, v_ref, pg_ref, na_ref, dtb_ref, stat_ref,
         of_ref, og_ref, ng_ref, _merged_in, o_ref, s_ref, gb_ref, bb_ref) = refs
    else:
        q_ref, k_ref, v_ref, pg_ref, na_ref, dtb_ref, stat_ref, o_ref, s_ref, gb_ref, bb_ref = refs
    c = CHUNK
    ngroups = nchunk // unroll
    hg = pl.program_id(1)

    @pl.when(pl.program_id(2) == 0)
    def _():
        s_ref[...] = jnp.zeros_like(s_ref)

    incl = stat_ref[0]
    strict = stat_ref[1]
    eye = stat_ref[2]
    incl_b = incl.astype(BF16)
    ones_b = stat_ref[3].astype(BF16)
    incl_t = 1.0 - strict
    n_lvl = stat_ref.shape[0] - 5

    pg = pg_ref[...]
    glog = na_ref[...] * _softplus(pg + dtb_ref[...])
    beta = _sigmoid(pg)
    cum_all = jnp.concatenate(
        [_dot_split_lhs(incl_b, glog[ci * c:(ci + 1) * c]) for ci in range(nchunk)], axis=0)
    rsel = lax.broadcasted_iota(jnp.int32, (LANE, LANE), 0)
    d = 1 if rev else 0
    for hh in range(hb):
        h = hg * hb + hh
        gb_ref[hh] = _dot_split_rhs(cum_all, jnp.where(rsel == d * heads + h, 1.0, 0.0).astype(BF16))
        bb_ref[hh] = _dot_split_rhs(beta, jnp.where(rsel == (2 + d) * heads + h, 1.0, 0.0).astype(BF16))

    d8 = stat_ref[4]
    edge_row = 0 if rev else c - 1

    def group_step(gi, carry):
        rows_list = _group_rows(gi, ngroups, unroll, rev)
        items = [(rows, hh) for rows in rows_list for hh in range(hb)]
        ni = len(items)
        rng = range(ni)
        lanes = [slice(hh * LANE, (hh + 1) * LANE) for _, hh in items]
        q = [q_ref[rows, lanes[i]] for i, (rows, _) in enumerate(items)]
        k = [k_ref[rows, lanes[i]] for i, (rows, _) in enumerate(items)]
        v = [v_ref[rows, lanes[i]] for i, (rows, _) in enumerate(items)]
        cum = [gb_ref[hh, rows, :] for rows, hh in items]
        bb = [bb_ref[hh, rows, :] for rows, hh in items]
        cum_row = [_dot_split_lhs(ones_b, cum[i][:, 0:c] * eye) for i in rng]
        decay = [jnp.exp(jnp.where(incl > 0, cum[i][:, 0:c] - cum_row[i], -jnp.inf)) for i in rng]
        ecum = [jnp.exp(cum[i]) for i in rng]
        kb = [k[i] * bb[i] for i in rng]
        kbf = [k[i].astype(BF16) for i in rng]
        a = [strict * _dot_nt(kb[i].astype(BF16), kbf[i]) * decay[i] for i in rng]
        qk = [(_dot_nt(q[i].astype(BF16), kbf[i]) * decay[i]).astype(BF16) for i in rng]
        a0 = [a[i] * d8 for i in rng]
        a0b = [a0[i].astype(BF16) for i in rng]
        a2b = [_dot(a0b[i], a0b[i]).astype(BF16) for i in rng]
        xm = [eye - a0[i] for i in rng]
        ym = [xm[i] + _dot(xm[i].astype(BF16), a2b[i]) for i in rng]
        a4b = [_dot(a2b[i], a2b[i]).astype(BF16) for i in rng]
        tm = [ym[i] + _dot(ym[i].astype(BF16), a4b[i]) for i in rng]
        for lvl in range(n_lvl):
            off = stat_ref[5 + lvl]
            tmb = [tm[i].astype(BF16) for i in rng]
            mid = [_dot(tmb[i], (a[i] * off).astype(BF16)).astype(BF16) for i in rng]
            tm = [tm[i] - _dot(mid[i], tmb[i]) for i in rng]
        tmb = [tm[i].astype(BF16) for i in rng]
        u = [_dot(tmb[i], (v[i] * bb[i]).astype(BF16)) for i in rng]
        w = [_dot(tmb[i], (kb[i] * ecum[i]).astype(BF16)).astype(BF16) for i in rng]
        edge = [cum[i][edge_row:edge_row + 1] for i in rng]
        qd = [(q[i] * ecum[i]).astype(BF16) for i in rng]
        kd = [(k[i] * jnp.exp(edge[i] - cum[i])).astype(BF16) for i in rng]
        de = [jnp.exp(edge[i]) for i in rng]
        st = [s_ref[hh] for hh in range(hb)]
        heads_rng = range(hb)
        for ui, rows in enumerate(rows_list):
            idx = [ui * hb + hh for hh in heads_rng]
            stb = [st[hh].astype(BF16) for hh in heads_rng]
            ws = [_dot(w[idx[hh]], stb[hh]) for hh in heads_rng]
            oq = [_dot(qd[idx[hh]], stb[hh]) for hh in heads_rng]
            vnb = [(u[idx[hh]] - ws[hh]).astype(BF16) for hh in heads_rng]
            dst = [_dot_tn(kd[idx[hh]], vnb[hh]) for hh in heads_rng]
            o = [oq[hh] + _dot(qk[idx[hh]], vnb[hh]) for hh in heads_rng]
            st = [st[hh] * de[idx[hh]] + dst[hh] for hh in heads_rng]
            for hh in heads_rng:
                ln = slice(hh * LANE, (hh + 1) * LANE)
                if epilogue:
                    tot = o[hh] + of_ref[rows, ln]
                    y = tot * lax.rsqrt(jnp.mean(tot * tot, axis=-1, keepdims=True) + RMS_EPS) * ng_ref[...]
                    gate = og_ref[rows, ln]
                    o_ref[rows, ln] = (y * gate * _sigmoid(gate)).astype(o_ref.dtype)
                else:
                    o_ref[rows, ln] = o[hh]
        for hh in heads_rng:
            s_ref[hh] = st[hh]
        return carry

    lax.fori_loop(0, ngroups, group_step, 0)


def _gdn(qkv, pg, p3, o_fwd, merged, neg_a, dtb, norm_g, *, rev, heads, gcol, rows=512, unroll=4, hb=2):
    bsz, s, _ = qkv.shape
    rows = _pick(s, rows)
    ns = s // rows
    nchunk = rows // CHUNK
    unroll = math.gcd(unroll, nchunk)
    hb = math.gcd(hb, heads)
    stat = jnp.asarray(_gd_static(CHUNK, rev), F32)
    seq = (lambda i: ns - 1 - i) if rev else (lambda i: i)
    wide = hb * LANE

    def qspec(c0):
        return pl.BlockSpec((None, rows, wide), lambda b, h, i: (b, seq(i), c0 // hb + h))

    vec = pl.BlockSpec((1, LANE), lambda b, h, i: (0, 0))
    in_specs = [qspec(0), qspec(heads), qspec(2 * heads),
                pl.BlockSpec((None, rows, LANE), lambda b, h, i: (b, seq(i), 0)),
                vec, vec, pl.BlockSpec(stat.shape, lambda b, h, i: (0, 0, 0))]
    args = [qkv, qkv, qkv, pg, neg_a, dtb, stat]
    kwargs = {}
    if rev:
        in_specs += [qspec(0), qspec(gcol), vec, pl.BlockSpec(memory_space=pl.ANY)]
        args += [o_fwd, p3, norm_g, merged]
        out_shape = jax.ShapeDtypeStruct(merged.shape, merged.dtype)
        out_spec = qspec(heads)
        kwargs["input_output_aliases"] = {len(args) - 1: 0}
    else:
        out_shape = jax.ShapeDtypeStruct((bsz, s, heads * LANE), F32)
        out_spec = qspec(0)
    return pl.pallas_call(
        functools.partial(_gdn_kernel, rev=rev, nchunk=nchunk, unroll=unroll, epilogue=rev, heads=heads, hb=hb),
        grid=(bsz, heads // hb, ns),
        in_specs=in_specs,
        out_specs=out_spec,
        out_shape=out_shape,
        scratch_shapes=[pltpu.VMEM((hb, HEAD_DIM, HEAD_DIM), F32),
                        pltpu.VMEM((hb, rows, LANE), F32), pltpu.VMEM((hb, rows, LANE), F32)],
        compiler_params=_cparams(("parallel", "parallel", "arbitrary")),
        name="gdn_bwd" if rev else "gdn_fwd",
        **kwargs,
    )(*args)


def _pad_cols(w, n):
    return jnp.pad(w, ((0, 0),) * (w.ndim - 1) + ((0, n - w.shape[-1]),))


def kernel(x, ln_in_g, ln_in_b, w_in, hg_lb_param, gd_conv_w, gd_a_log, gd_dt_bias, hg_norm_g, gd_norm_g, w_out,
           ln1_g, ln1_b, ffn_w_gate, ffn_w_up, ffn_w_down, moe_router, moe_w_gate, moe_w_up, moe_w_down,
           ln2_g, ln2_b):
    bsz, s, d = x.shape
    depth = w_in.shape[0]
    m = bsz * s
    half = d // 2
    heads = half // HEAD_DIM
    alpha = (2 * depth) ** 0.25
    n_main = 8 * half
    n_gate = N_GATE_COLS * heads
    names = ("hg_q", "hg_f_fwd", "hg_f_bwd", "hg_i", "hg_g", "gd_q", "gd_k", "gd_v", "gd_g")
    col = {nm: i * heads for i, nm in enumerate(names)}

    lb = jnp.cumsum(jax.nn.softmax(hg_lb_param.astype(F32), axis=0), axis=0)
    lb = lb - lb[0:1]
    log_lb = jnp.log(lb).reshape(depth, 2, heads, 1, LANE)
    log_1m_lb = jnp.log1p(-lb).reshape(depth, 2, heads, 1, LANE)

    xf, xb = _layer_norm(x.reshape(m, d), None, ln_in_g, ln_in_b)
    for l in range(depth):
        w_l = w_in[l]
        w_main = jnp.concatenate([w_l[:, :n_main], w_l[:, n_main + n_gate:]], axis=1).astype(BF16)
        w_gate = _pad_cols(w_l[:, n_main:n_main + n_gate], LANE).astype(BF16)
        p3 = _matmul(xb, w_main, F32, name="in_proj").reshape(bsz, s, -1)
        pg = _matmul(xb, w_gate, F32, tn=LANE, name="gate_proj").reshape(bsz, s, LANE)

        o_f = _hgrn2(p3, None, log_lb[l, 0], log_1m_lb[l, 0], None, rev=False, heads=heads, col=col, d_model=d)
        merged = _hgrn2(p3, o_f, log_lb[l, 1], log_1m_lb[l, 1], hg_norm_g[l].reshape(1, LANE).astype(F32),
                        rev=True, heads=heads, col=col, d_model=d)

        qkv = _gd_prep(p3, gd_conv_w[l].astype(F32), col0=col["gd_q"], heads=heads)
        neg_a = jnp.zeros((1, LANE), F32).at[0, :2 * heads].set(-jnp.exp(gd_a_log[l].astype(F32)).reshape(-1))
        dtb = jnp.zeros((1, LANE), F32).at[0, :2 * heads].set(gd_dt_bias[l].astype(F32).reshape(-1))
        gn = gd_norm_g[l].reshape(1, LANE).astype(F32)
        g_f = _gdn(qkv, pg, None, None, None, neg_a, dtb, None, rev=False, heads=heads, gcol=col["gd_g"])
        merged = _gdn(qkv, pg, p3, g_f, merged, neg_a, dtb, gn, rev=True, heads=heads, gcol=col["gd_g"])

        mix = _matmul(merged.reshape(m, d), w_out[l].astype(BF16), F32, name="out_proj")
        xf, xb = _layer_norm(xf, mix, ln1_g[l], ln1_b[l], alpha=alpha)

        j = l // 2
        if l % 2 == 0:
            f = ffn_w_gate.shape[-1]
            fp = -(-f // 1024) * 1024
            hdn = _swiglu(xb, _pad_cols(ffn_w_gate[j], fp).astype(BF16)[None],
                          _pad_cols(ffn_w_up[j], fp).astype(BF16)[None])
            wd = jnp.pad(ffn_w_down[j], ((0, fp - f), (0, 0))).astype(BF16)
        else:
            gates = _router_gates(xf, moe_router[j])
            hdn = _swiglu(xb, moe_w_gate[j].astype(BF16), moe_w_up[j].astype(BF16), gates=gates)
            wd = moe_w_down[j].astype(BF16).reshape(-1, d)
        ffn = _matmul(hdn, wd, F32, tn=1024, tk=2048 if l % 2 else 1408, name="down_proj")
        xf, xb = _layer_norm(xf, ffn, ln2_g[l], ln2_b[l], alpha=alpha)
    return xf.reshape(bsz, s, d)
```

```python
import functools
import math

import numpy as np
import jax
import jax.numpy as jnp
from jax import lax
from jax.experimental import pallas as pl
from jax.experimental.pallas import tpu as pltpu

F32 = jnp.float32
BF16 = jnp.bfloat16

LANE = 128
HEAD_DIM = 128
CHUNK = 64
CONV_K = 5
N_GATE_COLS = 4
LN_EPS = 1e-5
RMS_EPS = 1e-6
VMEM_LIMIT_BYTES = 56 * 1024 * 1024


def _cparams(sem):
    return pltpu.CompilerParams(dimension_semantics=sem, vmem_limit_bytes=VMEM_LIMIT_BYTES)


def _pick(n, pref):
    t = min(n, pref)
    while n % t:
        t -= LANE
    return t


def _mm_kernel(a_ref, b_ref, o_ref):
    o_ref[...] = jnp.dot(a_ref[...], b_ref[...], preferred_element_type=F32).astype(o_ref.dtype)


def _mm_acc_kernel(a_ref, b_ref, o_ref, acc_ref, *, nk):
    k = pl.program_id(2)
    prod = jnp.dot(a_ref[...], b_ref[...], preferred_element_type=F32)

    @pl.when(k == 0)
    def _():
        acc_ref[...] = prod

    @pl.when(k > 0)
    def _():
        acc_ref[...] += prod

    @pl.when(k == nk - 1)
    def _():
        o_ref[...] = acc_ref[...].astype(o_ref.dtype)


def _matmul(a, b, out_dtype, tm=1024, tn=512, tk=None, name="matmul"):
    m, kdim = a.shape
    n = b.shape[1]
    tm, tn = _pick(m, tm), _pick(n, tn)
    tk = kdim if tk is None else _pick(kdim, tk)
    nk = kdim // tk
    if nk == 1:
        return pl.pallas_call(
            _mm_kernel,
            grid=(m // tm, n // tn),
            in_specs=[pl.BlockSpec((tm, kdim), lambda i, j: (i, 0)),
                      pl.BlockSpec((kdim, tn), lambda i, j: (0, j))],
            out_specs=pl.BlockSpec((tm, tn), lambda i, j: (i, j)),
            out_shape=jax.ShapeDtypeStruct((m, n), out_dtype),
            compiler_params=_cparams(("parallel", "arbitrary")),
            name=name,
        )(a, b)
    return pl.pallas_call(
        functools.partial(_mm_acc_kernel, nk=nk),
        grid=(m // tm, n // tn, nk),
        in_specs=[pl.BlockSpec((tm, tk), lambda i, j, k: (i, k)),
                  pl.BlockSpec((tk, tn), lambda i, j, k: (k, j))],
        out_specs=pl.BlockSpec((tm, tn), lambda i, j, k: (i, j)),
        out_shape=jax.ShapeDtypeStruct((m, n), out_dtype),
        scratch_shapes=[pltpu.VMEM((tm, tn), F32)],
        compiler_params=_cparams(("parallel", "arbitrary", "arbitrary")),
        name=name,
    )(a, b)


def _mm2_kernel(a1_ref, a2_ref, b1_ref, b2_ref, o_ref):
    acc = jnp.dot(a1_ref[...], b1_ref[...], preferred_element_type=F32)
    acc = acc + jnp.dot(a2_ref[...], b2_ref[...], preferred_element_type=F32)
    o_ref[...] = acc.astype(o_ref.dtype)


def _matmul_cat(a1, a2, b, out_dtype, tm=1024, tn=512, name="matmul_cat"):
    m, kh = a1.shape
    n = b.shape[1]
    tm, tn = _pick(m, tm), _pick(n, tn)
    return pl.pallas_call(
        _mm2_kernel,
        grid=(m // tm, n // tn),
        in_specs=[pl.BlockSpec((tm, kh), lambda i, j: (i, 0)), pl.BlockSpec((tm, kh), lambda i, j: (i, 0)),
                  pl.BlockSpec((kh, tn), lambda i, j: (0, j)), pl.BlockSpec((kh, tn), lambda i, j: (1, j))],
        out_specs=pl.BlockSpec((tm, tn), lambda i, j: (i, j)),
        out_shape=jax.ShapeDtypeStruct((m, n), out_dtype),
        compiler_params=_cparams(("parallel", "arbitrary")),
        name=name,
    )(a1, a2, b, b)


def _sigmoid(x):
    return 1.0 / (1.0 + jnp.exp(-x))


def _swiglu_kernel(a_ref, wg_ref, wu_ref, o_ref):
    a = a_ref[...]
    g = jnp.dot(a, wg_ref[...], preferred_element_type=F32)
    u = jnp.dot(a, wu_ref[...], preferred_element_type=F32)
    o_ref[...] = (g * _sigmoid(g) * u).astype(o_ref.dtype)


def _swiglu(a, wg, wu, tm=1024, tn=512, name="swiglu"):
    m, kdim = a.shape
    f = wg.shape[1]
    tm, tn = _pick(m, tm), _pick(f, tn)
    wspec = pl.BlockSpec((kdim, tn), lambda i, j: (0, j))
    return pl.pallas_call(
        _swiglu_kernel,
        grid=(m // tm, f // tn),
        in_specs=[pl.BlockSpec((tm, kdim), lambda i, j: (i, 0)), wspec, wspec],
        out_specs=pl.BlockSpec((tm, tn), lambda i, j: (i, j)),
        out_shape=jax.ShapeDtypeStruct((m, f), BF16),
        compiler_params=_cparams(("parallel", "arbitrary")),
        name=name,
    )(a, wg, wu)


def _ln_kernel(*refs, alpha, has_res):
    if has_res:
        x_ref, r_ref, g_ref, b_ref, of_ref, ob_ref = refs
        x = alpha * x_ref[...] + r_ref[...]
    else:
        x_ref, g_ref, b_ref, of_ref, ob_ref = refs
        x = x_ref[...]
    mu = jnp.mean(x, axis=-1, keepdims=True)
    xc = x - mu
    var = jnp.mean(xc * xc, axis=-1, keepdims=True)
    y = xc * lax.rsqrt(var + LN_EPS) * g_ref[...] + b_ref[...]
    of_ref[...] = y
    ob_ref[...] = y.astype(BF16)


def _layer_norm(x, res, g, b, alpha=1.0, tm=256):
    m, d = x.shape
    tm = _pick(m, tm)
    row = pl.BlockSpec((tm, d), lambda i: (i, 0))
    vec = pl.BlockSpec((1, d), lambda i: (0, 0))
    args = [x] + ([res] if res is not None else []) + [g.reshape(1, d), b.reshape(1, d)]
    return pl.pallas_call(
        functools.partial(_ln_kernel, alpha=alpha, has_res=res is not None),
        grid=(m // tm,),
        in_specs=[row] * (len(args) - 2) + [vec, vec],
        out_specs=[row, row],
        out_shape=[jax.ShapeDtypeStruct((m, d), F32), jax.ShapeDtypeStruct((m, d), BF16)],
        compiler_params=_cparams(("parallel",)),
        name="layer_norm",
    )(*args)


def _router_kernel(x_ref, r_ref, o_ref, *, n_experts):
    logits = jnp.dot(x_ref[...], r_ref[...], precision=lax.Precision.HIGHEST, preferred_element_type=F32)
    lane = lax.broadcasted_iota(jnp.int32, logits.shape, 1).astype(F32)
    neg = -jnp.inf
    lg = jnp.where(lane < n_experts, logits, neg)
    m1 = jnp.max(lg, axis=1, keepdims=True)
    i1 = jnp.min(jnp.where(lg == m1, lane, float(LANE)), axis=1, keepdims=True)
    lg2 = jnp.where(lane == i1, neg, lg)
    m2 = jnp.max(lg2, axis=1, keepdims=True)
    i2 = jnp.min(jnp.where(lg2 == m2, lane, float(LANE)), axis=1, keepdims=True)
    e2 = jnp.exp(m2 - m1)
    w1 = 1.0 / (1.0 + e2)
    o_ref[...] = (jnp.where(lane == 0.0, i1, 0.0) + jnp.where(lane == 1.0, i2, 0.0)
                  + jnp.where(lane == 2.0, w1, 0.0) + jnp.where(lane == 3.0, e2 * w1, 0.0))


def _router_top2(x, router, tm=512):
    m, d = x.shape
    ne = router.shape[1]
    tm = _pick(m, tm)
    rpad = jnp.zeros((d, LANE), F32).at[:, :ne].set(router.astype(F32))
    return pl.pallas_call(
        functools.partial(_router_kernel, n_experts=ne),
        grid=(m // tm,),
        in_specs=[pl.BlockSpec((tm, d), lambda i: (i, 0)), pl.BlockSpec((d, LANE), lambda i: (0, 0))],
        out_specs=pl.BlockSpec((tm, LANE), lambda i: (i, 0)),
        out_shape=jax.ShapeDtypeStruct((m, LANE), F32),
        compiler_params=_cparams(("parallel",)),
        name="router",
    )(x, rpad)


MOE_TILE = 512
MOE_TOKENS = 256
TOP_K = 2


def _route_plan(top2, n_experts, tile):
    m = top2.shape[0]
    ef = top2[:, 0:TOP_K].astype(jnp.int32).reshape(-1)
    oh = (ef[:, None] == jnp.arange(n_experts, dtype=jnp.int32)[None, :]).astype(jnp.int32)
    rank = jnp.cumsum(oh, axis=0) - oh
    counts = jnp.sum(oh, axis=0)
    padded = ((counts + tile - 1) // tile) * tile
    ends = jnp.cumsum(padded)
    starts = ends - padded
    dest = jnp.sum(oh * (starts[None, :] + rank), axis=1).astype(jnp.int32)
    n_tiles = (TOP_K * m) // tile + n_experts
    tile_start = jnp.arange(n_tiles, dtype=jnp.int32) * tile
    tile_expert = jnp.sum((tile_start[:, None] >= ends[None, :]).astype(jnp.int32), axis=1)
    tile_expert = jnp.minimum(tile_expert, n_experts - 1).astype(jnp.int32)
    n_used = (ends[-1] // tile).astype(jnp.int32).reshape(1)
    return dest, tile_expert, n_used, n_tiles


def _dispatch_kernel(dest_ref, x_ref, _xg_in, xg_ref, sem, *, tq):
    base = pl.program_id(0) * tq

    def row_copy(r, slot):
        row = dest_ref[TOP_K * (base + r) + slot]
        return pltpu.make_async_copy(x_ref.at[pl.ds(r, 1)], xg_ref.at[pl.ds(row, 1)], sem)

    def issue(r, carry):
        for slot in range(TOP_K):
            row_copy(r, slot).start()
        return carry

    lax.fori_loop(0, tq, issue, 0)
    for _ in range(TOP_K):
        pltpu.make_async_copy(x_ref, xg_ref.at[pl.ds(0, tq)], sem).wait()


def _dispatch(xf, dest, n_rows, tq=MOE_TOKENS):
    m, d = xf.shape
    tq = _pick(m, tq)
    return pl.pallas_call(
        functools.partial(_dispatch_kernel, tq=tq),
        grid_spec=pltpu.PrefetchScalarGridSpec(
            num_scalar_prefetch=1,
            grid=(m // tq,),
            in_specs=[pl.BlockSpec((tq, d), lambda i, dest: (i, 0)), pl.BlockSpec(memory_space=pl.ANY)],
            out_specs=pl.BlockSpec(memory_space=pl.ANY),
            scratch_shapes=[pltpu.SemaphoreType.DMA(())],
        ),
        out_shape=jax.ShapeDtypeStruct((n_rows, d), F32),
        input_output_aliases={2: 0},
        compiler_params=_cparams(("arbitrary",)),
        name="moe_dispatch",
    )(dest, xf, jnp.zeros((n_rows, d), F32))


def _gswiglu_kernel(te_ref, nu_ref, a_ref, wg_ref, wu_ref, o_ref, abf_ref):
    i = pl.program_id(0)

    @pl.when(pl.program_id(1) == 0)
    def _():
        abf_ref[...] = a_ref[...].astype(BF16)

    @pl.when(i < nu_ref[0])
    def _():
        a = abf_ref[...]
        g = jnp.dot(a, wg_ref[...], preferred_element_type=F32)
        u = jnp.dot(a, wu_ref[...], preferred_element_type=F32)
        o_ref[...] = (g * _sigmoid(g) * u).astype(o_ref.dtype)

    @pl.when(i >= nu_ref[0])
    def _():
        o_ref[...] = jnp.zeros_like(o_ref)


def _gdown_kernel(te_ref, nu_ref, a_ref, w_ref, o_ref):
    i = pl.program_id(0)

    @pl.when(i < nu_ref[0])
    def _():
        o_ref[...] = jnp.dot(a_ref[...], w_ref[...], preferred_element_type=F32)

    @pl.when(i >= nu_ref[0])
    def _():
        o_ref[...] = jnp.zeros_like(o_ref)


def _grouped_experts(xg, wg, wu, wd, tile_expert, n_used, tile=MOE_TILE, tn=512, tn_down=1024):
    p, d = xg.shape
    f = wg.shape[2]
    tn, tn_down = _pick(f, tn), _pick(d, tn_down)
    nt = p // tile

    def wspec(rows, cols):
        return pl.BlockSpec((None, rows, cols),
                            lambda i, j, te, nu: (te[i], 0, jnp.where(i < nu[0], j, 0)))

    hdn = pl.pallas_call(
        _gswiglu_kernel,
        grid_spec=pltpu.PrefetchScalarGridSpec(
            num_scalar_prefetch=2,
            grid=(nt, f // tn),
            in_specs=[pl.BlockSpec((tile, d), lambda i, j, te, nu: (i, 0)), wspec(d, tn), wspec(d, tn)],
            out_specs=pl.BlockSpec((tile, tn), lambda i, j, te, nu: (i, j)),
            scratch_shapes=[pltpu.VMEM((tile, d), BF16)],
        ),
        out_shape=jax.ShapeDtypeStruct((p, f), BF16),
        compiler_params=_cparams(("arbitrary", "arbitrary")),
        name="moe_swiglu",
    )(tile_expert, n_used, xg, wg, wu)
    return pl.pallas_call(
        _gdown_kernel,
        grid_spec=pltpu.PrefetchScalarGridSpec(
            num_scalar_prefetch=2,
            grid=(nt, d // tn_down),
            in_specs=[pl.BlockSpec((tile, f), lambda i, j, te, nu: (i, 0)), wspec(f, tn_down)],
            out_specs=pl.BlockSpec((tile, tn_down), lambda i, j, te, nu: (i, j)),
        ),
        out_shape=jax.ShapeDtypeStruct((p, d), F32),
        compiler_params=_cparams(("arbitrary", "arbitrary")),
        name="moe_down",
    )(tile_expert, n_used, hdn, wd)


def _combine_ln_kernel(dest_ref, x_ref, top2_ref, g_ref, b_ref, yg_ref, of_ref, ob_ref, buf_ref, sem, *, alpha, tq):
    base = pl.program_id(0) * tq

    def row_copy(r, slot):
        row = dest_ref[TOP_K * (base + r) + slot]
        return pltpu.make_async_copy(yg_ref.at[pl.ds(row, 1)], buf_ref.at[slot, pl.ds(r, 1)], sem)

    def issue(r, carry):
        for slot in range(TOP_K):
            row_copy(r, slot).start()
        return carry

    lax.fori_loop(0, tq, issue, 0)
    for slot in range(TOP_K):
        pltpu.make_async_copy(yg_ref.at[pl.ds(0, tq)], buf_ref.at[slot], sem).wait()
    top2 = top2_ref[...]
    ffn = top2[:, TOP_K:TOP_K + 1] * buf_ref[0]
    for slot in range(1, TOP_K):
        ffn = ffn + top2[:, TOP_K + slot:TOP_K + slot + 1] * buf_ref[slot]
    x = alpha * x_ref[...] + ffn
    mu = jnp.mean(x, axis=-1, keepdims=True)
    xc = x - mu
    var = jnp.mean(xc * xc, axis=-1, keepdims=True)
    y = xc * lax.rsqrt(var + LN_EPS) * g_ref[...] + b_ref[...]
    of_ref[...] = y
    ob_ref[...] = y.astype(BF16)


def _combine_layer_norm(x, yg, top2, dest, g, b, alpha, tq=MOE_TOKENS):
    m, d = x.shape
    tq = _pick(m, tq)
    row = pl.BlockSpec((tq, d), lambda i, dest: (i, 0))
    vec = pl.BlockSpec((1, d), lambda i, dest: (0, 0))
    return pl.pallas_call(
        functools.partial(_combine_ln_kernel, alpha=alpha, tq=tq),
        grid_spec=pltpu.PrefetchScalarGridSpec(
            num_scalar_prefetch=1,
            grid=(m // tq,),
            in_specs=[row, pl.BlockSpec((tq, LANE), lambda i, dest: (i, 0)), vec, vec,
                      pl.BlockSpec(memory_space=pl.ANY)],
            out_specs=[row, row],
            scratch_shapes=[pltpu.VMEM((TOP_K, tq, d), F32), pltpu.SemaphoreType.DMA(())],
        ),
        out_shape=[jax.ShapeDtypeStruct((m, d), F32), jax.ShapeDtypeStruct((m, d), BF16)],
        compiler_params=_cparams(("arbitrary",)),
        name="moe_combine_ln",
    )(dest, x, top2, g.reshape(1, d), b.reshape(1, d), yg)


def _dot(a, b):
    return jnp.dot(a, b, preferred_element_type=F32)


def _dot_nt(a, b):
    return lax.dot_general(a, b, (((1,), (1,)), ((), ())), preferred_element_type=F32)


def _dot_tn(a, b):
    return lax.dot_general(a, b, (((0,), (0,)), ((), ())), preferred_element_type=F32)


def _split2(x):
    hi = x.astype(BF16)
    return hi, (x - hi.astype(F32)).astype(BF16)


def _dot_split_lhs(stat, x):
    hi, lo = _split2(x)
    return _dot(stat, hi) + _dot(stat, lo)


def _dot_split_rhs(x, stat):
    hi, lo = _split2(x)
    return _dot(hi, stat) + _dot(lo, stat)


def _dotb(a, b):
    return _dot(a.astype(BF16), b.astype(BF16))


def _log_sigmoid(z):
    return jnp.minimum(z, 0.0) - jnp.log(1.0 + jnp.exp(-jnp.abs(z)))


def _softplus(z):
    return jnp.maximum(z, 0.0) + jnp.log(1.0 + jnp.exp(-jnp.abs(z)))


def _n_levels(c):
    return int(round(math.log2(c)))


def _group_rows(gi, ngroups, unroll, rev):
    gg = (ngroups - 1 - gi) if rev else gi
    order = range(unroll - 1, -1, -1) if rev else range(unroll)
    return [pl.ds(pl.multiple_of((gg * unroll + u) * CHUNK, CHUNK), CHUNK) for u in order]


@functools.lru_cache(maxsize=None)
def _hg_static(c, rev):
    n = _n_levels(c)
    t = np.arange(c)[:, None]
    r = np.arange(c)[None, :]
    mats = [r <= t]
    d_mats, e_mats, masks = [], [], [t == r]
    for lvl in range(1, n + 1):
        blk, half = 1 << lvl, 1 << (lvl - 1)
        b0 = (t // blk) * blk
        mid = b0 + half - 1
        second = (t - b0) >= half
        d_mats.append(second & (r > mid) & (r <= t))
        e_mats.append((~second) & (r > t) & (r <= mid))
        masks.append((b0 == (r // blk) * blk) & second & ((r % blk) < half))
    mats = mats + d_mats + e_mats + [r > t]
    if rev:
        mats = [m[::-1, ::-1] for m in mats]
        masks = [m[::-1, ::-1] for m in masks]
    stat = np.concatenate(mats, axis=0).astype(np.float32)
    return stat, np.stack(masks).astype(np.float32)


def _hgrn2_kernel(*refs, rev, nchunk, unroll, epilogue, scale):
    if epilogue:
        (q_ref, z_ref, v_ref, llb_ref, l1m_ref, stat_ref, mask_ref,
         of_ref, og_ref, ng_ref, o_ref, st_ref) = refs
    else:
        q_ref, z_ref, v_ref, llb_ref, l1m_ref, stat_ref, mask_ref, o_ref, st_ref = refs
    c = CHUNK
    n = _n_levels(c)
    ngroups = nchunk // unroll

    @pl.when(pl.program_id(2) == 0)
    def _():
        st_ref[...] = jnp.zeros_like(st_ref)

    log_lb = llb_ref[...]
    log_1m_lb = l1m_ref[...]

    def group_step(gi, carry):
        rows_list = _group_rows(gi, ngroups, unroll, rev)
        nu = len(rows_list)
        q = [q_ref[rows, :] * scale for rows in rows_list]
        vb = [v_ref[rows, :].astype(BF16) for rows in rows_list]
        b = [log_1m_lb + _log_sigmoid(z_ref[rows, :]) for rows in rows_list]
        g = [jnp.maximum(log_lb, bi) + jnp.log(1.0 + jnp.exp(-jnp.abs(log_lb - bi))) for bi in b]
        k = [1.0 - jnp.exp(gi_) for gi_ in g]
        xall = _dot_split_lhs(stat_ref[...], jnp.concatenate(g, axis=1))

        def expo(blk, i):
            return xall[blk * c:(blk + 1) * c, i * LANE:(i + 1) * LANE]

        att = [mask_ref[0] * _dot_nt(q[i].astype(BF16), k[i].astype(BF16)) for i in range(nu)]
        for lvl in range(1, n + 1):
            ql = [(q[i] * jnp.exp(expo(lvl, i))).astype(BF16) for i in range(nu)]
            kl = [(k[i] * jnp.exp(expo(n + lvl, i))).astype(BF16) for i in range(nu)]
            att = [att[i] + mask_ref[lvl] * _dot_nt(ql[i], kl[i]) for i in range(nu)]
        o_intra = [_dot(att[i].astype(BF16), vb[i]) for i in range(nu)]
        qd = [(q[i] * jnp.exp(expo(0, i))).astype(BF16) for i in range(nu)]
        kd = [(k[i] * jnp.exp(expo(2 * n + 1, i))).astype(BF16) for i in range(nu)]
        dst = [_dot_tn(vb[i], kd[i]) for i in range(nu)]
        edge_row = 0 if rev else c - 1
        de = [jnp.exp(expo(0, i)[edge_row:edge_row + 1]) for i in range(nu)]
        states = [st_ref[...]]
        for i in range(nu):
            states.append(states[i] * de[i] + dst[i])
        st_ref[...] = states[nu]
        outs = [o_intra[i] + _dot_nt(qd[i], states[i].astype(BF16)) for i in range(nu)]
        for rows, o in zip(rows_list, outs):
            if epilogue:
                tot = o + of_ref[rows, :]
                y = tot * lax.rsqrt(jnp.mean(tot * tot, axis=-1, keepdims=True) + RMS_EPS) * ng_ref[...]
                gate = og_ref[rows, :]
                o_ref[rows, :] = (y * gate * _sigmoid(gate)).astype(o_ref.dtype)
            else:
                o_ref[rows, :] = o
        return carry

    lax.fori_loop(0, ngroups, group_step, 0)


def _hgrn2(p3, o_fwd, llb, l1m, norm_g, *, rev, heads, col, d_model, rows=512, unroll=8):
    bsz, s, _ = p3.shape
    rows = _pick(s, rows)
    ns = s // rows
    nchunk = rows // CHUNK
    unroll = math.gcd(unroll, nchunk)
    stat, masks = _hg_static(CHUNK, rev)
    stat = jnp.asarray(stat, BF16)
    masks = jnp.asarray(masks, F32)
    zcol = col["hg_f_bwd"] if rev else col["hg_f_fwd"]
    seq = (lambda i: ns - 1 - i) if rev else (lambda i: i)

    def pspec(c0):
        return pl.BlockSpec((None, rows, LANE), lambda b, h, i: (b, seq(i), c0 + h))

    vec = pl.BlockSpec((None, 1, LANE), lambda b, h, i: (h, 0, 0))
    in_specs = [pspec(col["hg_q"]), pspec(zcol), pspec(col["hg_i"]), vec, vec,
                pl.BlockSpec(stat.shape, lambda b, h, i: (0, 0)),
                pl.BlockSpec(masks.shape, lambda b, h, i: (0, 0, 0))]
    args = [p3, p3, p3, llb, l1m, stat, masks]
    if rev:
        in_specs += [pl.BlockSpec((None, rows, LANE), lambda b, h, i: (b, seq(i), h)),
                     pspec(col["hg_g"]),
                     pl.BlockSpec((1, LANE), lambda b, h, i: (0, 0))]
        args += [o_fwd, p3, norm_g]
        out_shape = jax.ShapeDtypeStruct((bsz, s, heads * LANE), BF16)
    else:
        out_shape = jax.ShapeDtypeStruct((bsz, s, heads * LANE), F32)
    return pl.pallas_call(
        functools.partial(_hgrn2_kernel, rev=rev, nchunk=nchunk, unroll=unroll, epilogue=rev,
                          scale=HEAD_DIM ** -0.5),
        grid=(bsz, heads, ns),
        in_specs=in_specs,
        out_specs=pl.BlockSpec((None, rows, LANE), lambda b, h, i: (b, seq(i), h)),
        out_shape=out_shape,
        scratch_shapes=[pltpu.VMEM((HEAD_DIM, HEAD_DIM), F32)],
        compiler_params=_cparams(("parallel", "parallel", "arbitrary")),
        name="hgrn2_bwd" if rev else "hgrn2_fwd",
    )(*args)


def _gd_prep_kernel(x_ref, w_ref, o_ref, xp_ref, *, s, tile, n_qk, n_q, scale_q):
    j = pl.program_id(1)
    halo = 8
    xp_ref[0:halo, :] = jnp.zeros((halo, LANE), F32)
    xp_ref[halo + s:2 * halo + s, :] = jnp.zeros((halo, LANE), F32)
    xp_ref[halo:halo + s, :] = x_ref[...]
    w = w_ref[...]
    inv_scale = jnp.where(j < n_q, scale_q, 1.0)

    def tile_step(ti, carry):
        r0 = pl.multiple_of(ti * tile, tile)
        win = xp_ref[pl.ds(r0, tile + 2 * halo), :]
        acc = jnp.zeros((tile, LANE), F32)
        for tap in range(CONV_K):
            off = halo + tap - CONV_K // 2
            acc = acc + w[tap:tap + 1, :] * win[off:off + tile, :]
        y = acc * _sigmoid(acc)
        inv = lax.rsqrt(jnp.sum(y * y, axis=-1, keepdims=True) + RMS_EPS) * inv_scale
        o_ref[pl.ds(r0, tile), :] = y * jnp.where(j < n_qk, inv, 1.0)
        return carry

    lax.fori_loop(0, s // tile, tile_step, 0)


def _gd_prep(p3, conv_w, *, col0, heads, tile=256):
    bsz, s, _ = p3.shape
    tile = _pick(s, tile)
    nblk = 3 * heads
    return pl.pallas_call(
        functools.partial(_gd_prep_kernel, s=s, tile=tile, n_qk=2 * heads, n_q=heads, scale_q=HEAD_DIM ** -0.5),
        grid=(bsz, nblk),
        in_specs=[pl.BlockSpec((None, s, LANE), lambda b, j: (b, 0, col0 + j)),
                  pl.BlockSpec((CONV_K, LANE), lambda b, j: (0, j))],
        out_specs=pl.BlockSpec((None, s, LANE), lambda b, j: (b, 0, j)),
        out_shape=jax.ShapeDtypeStruct((bsz, s, nblk * LANE), F32),
        scratch_shapes=[pltpu.VMEM((s + 16, LANE), F32)],
        compiler_params=_cparams(("parallel", "parallel")),
        name="gd_prep",
    )(p3, conv_w)


@functools.lru_cache(maxsize=None)
def _gd_static(c, rev):
    t = np.arange(c)[:, None]
    s = np.arange(c)[None, :]
    incl = (s >= t) if rev else (s <= t)
    strict = (s > t) if rev else (s < t)
    mats = [incl, strict, t == s, np.ones((c, c), bool), strict & ((t // 8) == (s // 8))]
    blk = 16
    while blk <= c:
        half = blk // 2
        same = (t // blk) == (s // blk)
        t2, s2 = (t % blk) >= half, (s % blk) >= half
        mats.append(same & ((~t2) & s2 if rev else t2 & (~s2)))
        blk *= 2
    return np.stack(mats).astype(np.float32)


def _gdn_kernel(*refs, rev, nchunk, unroll, epilogue, heads, hb):
    if epilogue:
        (q_ref, k_ref, v_ref, pg_ref, na_ref, dtb_ref, stat_ref,
         of_ref, og_ref, ng_ref, o_ref, s_ref, gb_ref, bb_ref) = refs
    else:
        q_ref, k_ref, v_ref, pg_ref, na_ref, dtb_ref, stat_ref, o_ref, s_ref, gb_ref, bb_ref = refs
    c = CHUNK
    ngroups = nchunk // unroll
    hg = pl.program_id(1)

    @pl.when(pl.program_id(2) == 0)
    def _():
        s_ref[...] = jnp.zeros_like(s_ref)

    incl = stat_ref[0]
    strict = stat_ref[1]
    eye = stat_ref[2]
    incl_b = incl.astype(BF16)
    ones_b = stat_ref[3].astype(BF16)
    incl_t = 1.0 - strict
    n_lvl = stat_ref.shape[0] - 5

    pg = pg_ref[...]
    glog = na_ref[...] * _softplus(pg + dtb_ref[...])
    beta = _sigmoid(pg)
    cum_all = jnp.concatenate(
        [_dot_split_lhs(incl_b, glog[ci * c:(ci + 1) * c]) for ci in range(nchunk)], axis=0)
    rsel = lax.broadcasted_iota(jnp.int32, (LANE, LANE), 0)
    d = 1 if rev else 0
    for hh in range(hb):
        h = hg * hb + hh
        gb_ref[hh] = _dot_split_rhs(cum_all, jnp.where(rsel == d * heads + h, 1.0, 0.0).astype(BF16))
        bb_ref[hh] = _dot_split_rhs(beta, jnp.where(rsel == (2 + d) * heads + h, 1.0, 0.0).astype(BF16))

    d8 = stat_ref[4]
    edge_row = 0 if rev else c - 1

    def group_step(gi, carry):
        rows_list = _group_rows(gi, ngroups, unroll, rev)
        items = [(rows, hh) for rows in rows_list for hh in range(hb)]
        ni = len(items)
        rng = range(ni)
        lanes = [slice(hh * LANE, (hh + 1) * LANE) for _, hh in items]
        q = [q_ref[rows, lanes[i]] for i, (rows, _) in enumerate(items)]
        k = [k_ref[rows, lanes[i]] for i, (rows, _) in enumerate(items)]
        v = [v_ref[rows, lanes[i]] for i, (rows, _) in enumerate(items)]
        cum = [gb_ref[hh, rows, :] for rows, hh in items]
        bb = [bb_ref[hh, rows, :] for rows, hh in items]
        cum_row = [_dot_split_lhs(ones_b, cum[i][:, 0:c] * eye) for i in rng]
        decay = [jnp.exp(jnp.where(incl > 0, cum[i][:, 0:c] - cum_row[i], -jnp.inf)) for i in rng]
        ecum = [jnp.exp(cum[i]) for i in rng]
        kb = [k[i] * bb[i] for i in rng]
        kbf = [k[i].astype(BF16) for i in rng]
        a = [strict * _dot_nt(kb[i].astype(BF16), kbf[i]) * decay[i] for i in rng]
        qk = [(_dot_nt(q[i].astype(BF16), kbf[i]) * decay[i]).astype(BF16) for i in rng]
        a0 = [a[i] * d8 for i in rng]
        a0b = [a0[i].astype(BF16) for i in rng]
        a2b = [_dot(a0b[i], a0b[i]).astype(BF16) for i in rng]
        xm = [eye - a0[i] for i in rng]
        ym = [xm[i] + _dot(xm[i].astype(BF16), a2b[i]) for i in rng]
        a4b = [_dot(a2b[i], a2b[i]).astype(BF16) for i in rng]
        tm = [ym[i] + _dot(ym[i].astype(BF16), a4b[i]) for i in rng]
        for lvl in range(n_lvl):
            off = stat_ref[5 + lvl]
            tmb = [tm[i].astype(BF16) for i in rng]
            mid = [_dot(tmb[i], (a[i] * off).astype(BF16)).astype(BF16) for i in rng]
            tm = [tm[i] - _dot(mid[i], tmb[i]) for i in rng]
        tmb = [tm[i].astype(BF16) for i in rng]
        u = [_dot(tmb[i], (v[i] * bb[i]).astype(BF16)) for i in rng]
        w = [_dot(tmb[i], (kb[i] * ecum[i]).astype(BF16)).astype(BF16) for i in rng]
        edge = [cum[i][edge_row:edge_row + 1] for i in rng]
        qd = [(q[i] * ecum[i]).astype(BF16) for i in rng]
        kd = [(k[i] * jnp.exp(edge[i] - cum[i])).astype(BF16) for i in rng]
        de = [jnp.exp(edge[i]) for i in rng]
        st = [s_ref[hh] for hh in range(hb)]
        heads_rng = range(hb)
        for ui, rows in enumerate(rows_list):
            idx = [ui * hb + hh for hh in heads_rng]
            stb = [st[hh].astype(BF16) for hh in heads_rng]
            ws = [_dot(w[idx[hh]], stb[hh]) for hh in heads_rng]
            oq = [_dot(qd[idx[hh]], stb[hh]) for hh in heads_rng]
            vnb = [(u[idx[hh]] - ws[hh]).astype(BF16) for hh in heads_rng]
            dst = [_dot_tn(kd[idx[hh]], vnb[hh]) for hh in heads_rng]
            o = [oq[hh] + _dot(qk[idx[hh]], vnb[hh]) for hh in heads_rng]
            st = [st[hh] * de[idx[hh]] + dst[hh] for hh in heads_rng]
            for hh in heads_rng:
                ln = slice(hh * LANE, (hh + 1) * LANE)
                if epilogue:
                    tot = o[hh] + of_ref[rows, ln]
                    y = tot * lax.rsqrt(jnp.mean(tot * tot, axis=-1, keepdims=True) + RMS_EPS) * ng_ref[...]
                    gate = og_ref[rows, ln]
                    o_ref[rows, ln] = (y * gate * _sigmoid(gate)).astype(o_ref.dtype)
                else:
                    o_ref[rows, ln] = o[hh]
        for hh in heads_rng:
            s_ref[hh] = st[hh]
        return carry

    lax.fori_loop(0, ngroups, group_step, 0)


def _gdn(qkv, pg, p3, o_fwd, neg_a, dtb, norm_g, *, rev, heads, gcol, rows=512, unroll=4, hb=4):
    bsz, s, _ = qkv.shape
    rows = _pick(s, rows)
    ns = s // rows
    nchunk = rows // CHUNK
    unroll = math.gcd(unroll, nchunk)
    hb = math.gcd(hb, heads)
    stat = jnp.asarray(_gd_static(CHUNK, rev), F32)
    seq = (lambda i: ns - 1 - i) if rev else (lambda i: i)
    wide = hb * LANE

    def qspec(c0):
        return pl.BlockSpec((None, rows, wide), lambda b, h, i: (b, seq(i), c0 // hb + h))

    vec = pl.BlockSpec((1, LANE), lambda b, h, i: (0, 0))
    in_specs = [qspec(0), qspec(heads), qspec(2 * heads),
                pl.BlockSpec((None, rows, LANE), lambda b, h, i: (b, seq(i), 0)),
                vec, vec, pl.BlockSpec(stat.shape, lambda b, h, i: (0, 0, 0))]
    args = [qkv, qkv, qkv, pg, neg_a, dtb, stat]
    if rev:
        in_specs += [qspec(0), qspec(gcol), vec]
        args += [o_fwd, p3, norm_g]
    out_shape = jax.ShapeDtypeStruct((bsz, s, heads * LANE), BF16 if rev else F32)
    out_spec = qspec(0)
    return pl.pallas_call(
        functools.partial(_gdn_kernel, rev=rev, nchunk=nchunk, unroll=unroll, epilogue=rev, heads=heads, hb=hb),
        grid=(bsz, heads // hb, ns),
        in_specs=in_specs,
        out_specs=out_spec,
        out_shape=out_shape,
        scratch_shapes=[pltpu.VMEM((hb, HEAD_DIM, HEAD_DIM), F32),
                        pltpu.VMEM((hb, rows, LANE), F32), pltpu.VMEM((hb, rows, LANE), F32)],
        compiler_params=_cparams(("parallel", "parallel", "arbitrary")),
        name="gdn_bwd" if rev else "gdn_fwd",
    )(*args)


def _pad_cols(w, n):
    return jnp.pad(w, ((0, 0),) * (w.ndim - 1) + ((0, n - w.shape[-1]),))


def kernel(x, ln_in_g, ln_in_b, w_in, hg_lb_param, gd_conv_w, gd_a_log, gd_dt_bias, hg_norm_g, gd_norm_g, w_out,
           ln1_g, ln1_b, ffn_w_gate, ffn_w_up, ffn_w_down, moe_router, moe_w_gate, moe_w_up, moe_w_down,
           ln2_g, ln2_b):
    bsz, s, d = x.shape
    depth = w_in.shape[0]
    m = bsz * s
    half = d // 2
    heads = half // HEAD_DIM
    alpha = (2 * depth) ** 0.25
    n_main = 8 * half
    n_gate = N_GATE_COLS * heads
    names = ("hg_q", "hg_f_fwd", "hg_f_bwd", "hg_i", "hg_g", "gd_q", "gd_k", "gd_v", "gd_g")
    col = {nm: i * heads for i, nm in enumerate(names)}

    lb = jnp.cumsum(jax.nn.softmax(hg_lb_param.astype(F32), axis=0), axis=0)
    lb = lb - lb[0:1]
    log_lb = jnp.log(lb).reshape(depth, 2, heads, 1, LANE)
    log_1m_lb = jnp.log1p(-lb).reshape(depth, 2, heads, 1, LANE)

    xf, xb = _layer_norm(x.reshape(m, d), None, ln_in_g, ln_in_b)
    for l in range(depth):
        w_l = w_in[l]
        w_main = jnp.concatenate([w_l[:, :n_main], w_l[:, n_main + n_gate:]], axis=1).astype(BF16)
        w_gate = _pad_cols(w_l[:, n_main:n_main + n_gate], LANE).astype(BF16)
        p3 = _matmul(xb, w_main, F32, name="in_proj").reshape(bsz, s, -1)
        pg = _matmul(xb, w_gate, F32, tn=LANE, name="gate_proj").reshape(bsz, s, LANE)

        o_f = _hgrn2(p3, None, log_lb[l, 0], log_1m_lb[l, 0], None, rev=False, heads=heads, col=col, d_model=d)
        hg_out = _hgrn2(p3, o_f, log_lb[l, 1], log_1m_lb[l, 1], hg_norm_g[l].reshape(1, LANE).astype(F32),
                        rev=True, heads=heads, col=col, d_model=d)

        qkv = _gd_prep(p3, gd_conv_w[l].astype(F32), col0=col["gd_q"], heads=heads)
        neg_a = jnp.zeros((1, LANE), F32).at[0, :2 * heads].set(-jnp.exp(gd_a_log[l].astype(F32)).reshape(-1))
        dtb = jnp.zeros((1, LANE), F32).at[0, :2 * heads].set(gd_dt_bias[l].astype(F32).reshape(-1))
        gn = gd_norm_g[l].reshape(1, LANE).astype(F32)
        g_f = _gdn(qkv, pg, None, None, neg_a, dtb, None, rev=False, heads=heads, gcol=col["gd_g"])
        gd_out = _gdn(qkv, pg, p3, g_f, neg_a, dtb, gn, rev=True, heads=heads, gcol=col["gd_g"])

        mix = _matmul_cat(hg_out.reshape(m, half), gd_out.reshape(m, half), w_out[l].astype(BF16), F32,
                          name="out_proj")
        xf, xb = _layer_norm(xf, mix, ln1_g[l], ln1_b[l], alpha=alpha)

        j = l // 2
        if l % 2 == 0:
            f = ffn_w_gate.shape[-1]
            fp = -(-f // 1024) * 1024
            hdn = _swiglu(xb, _pad_cols(ffn_w_gate[j], fp).astype(BF16), _pad_cols(ffn_w_up[j], fp).astype(BF16))
            wd = jnp.pad(ffn_w_down[j], ((0, fp - f), (0, 0))).astype(BF16)
            ffn = _matmul(hdn, wd, F32, tn=1024, tk=1408, name="down_proj")
            xf, xb = _layer_norm(xf, ffn, ln2_g[l], ln2_b[l], alpha=alpha)
        else:
            n_experts = moe_router.shape[-1]
            top2 = _router_top2(xf, moe_router[j])
            dest, tile_expert, n_used, n_tiles = _route_plan(top2, n_experts, MOE_TILE)
            xg = _dispatch(xf, dest, n_tiles * MOE_TILE)
            yg = _grouped_experts(xg, moe_w_gate[j].astype(BF16), moe_w_up[j].astype(BF16),
                                  moe_w_down[j].astype(BF16), tile_expert, n_used)
            xf, xb = _combine_layer_norm(xf, yg, top2, dest, ln2_g[l], ln2_b[l], alpha)
    return xf.reshape(bsz, s, d)
```

```python
import functools
import math

import numpy as np
import jax
import jax.numpy as jnp
from jax import lax
from jax.experimental import pallas as pl
from jax.experimental.pallas import tpu as pltpu

F32 = jnp.float32
BF16 = jnp.bfloat16

LANE = 128
HEAD_DIM = 128
CHUNK = 64
CONV_K = 5
N_GATE_COLS = 4
LN_EPS = 1e-5
RMS_EPS = 1e-6
VMEM_LIMIT_BYTES = 56 * 1024 * 1024


def _cparams(sem):
    return pltpu.CompilerParams(dimension_semantics=sem, vmem_limit_bytes=VMEM_LIMIT_BYTES)


def _pick(n, pref):
    t = min(n, pref)
    while n % t:
        t -= LANE
    return t


def _mm_kernel(a_ref, b_ref, o_ref):
    o_ref[...] = jnp.dot(a_ref[...], b_ref[...], preferred_element_type=F32).astype(o_ref.dtype)


def _mm_acc_kernel(a_ref, b_ref, o_ref, acc_ref, *, nk):
    k = pl.program_id(2)
    prod = jnp.dot(a_ref[...], b_ref[...], preferred_element_type=F32)

    @pl.when(k == 0)
    def _():
        acc_ref[...] = prod

    @pl.when(k > 0)
    def _():
        acc_ref[...] += prod

    @pl.when(k == nk - 1)
    def _():
        o_ref[...] = acc_ref[...].astype(o_ref.dtype)


def _matmul(a, b, out_dtype, tm=1024, tn=512, tk=None, name="matmul"):
    m, kdim = a.shape
    n = b.shape[1]
    tm, tn = _pick(m, tm), _pick(n, tn)
    tk = kdim if tk is None else _pick(kdim, tk)
    nk = kdim // tk
    if nk == 1:
        return pl.pallas_call(
            _mm_kernel,
            grid=(m // tm, n // tn),
            in_specs=[pl.BlockSpec((tm, kdim), lambda i, j: (i, 0)),
                      pl.BlockSpec((kdim, tn), lambda i, j: (0, j))],
            out_specs=pl.BlockSpec((tm, tn), lambda i, j: (i, j)),
            out_shape=jax.ShapeDtypeStruct((m, n), out_dtype),
            compiler_params=_cparams(("parallel", "arbitrary")),
            name=name,
        )(a, b)
    return pl.pallas_call(
        functools.partial(_mm_acc_kernel, nk=nk),
        grid=(m // tm, n // tn, nk),
        in_specs=[pl.BlockSpec((tm, tk), lambda i, j, k: (i, k)),
                  pl.BlockSpec((tk, tn), lambda i, j, k: (k, j))],
        out_specs=pl.BlockSpec((tm, tn), lambda i, j, k: (i, j)),
        out_shape=jax.ShapeDtypeStruct((m, n), out_dtype),
        scratch_shapes=[pltpu.VMEM((tm, tn), F32)],
        compiler_params=_cparams(("parallel", "arbitrary", "arbitrary")),
        name=name,
    )(a, b)


def _mm2_kernel(a1_ref, a2_ref, b1_ref, b2_ref, o_ref):
    acc = jnp.dot(a1_ref[...], b1_ref[...], preferred_element_type=F32)
    acc = acc + jnp.dot(a2_ref[...], b2_ref[...], preferred_element_type=F32)
    o_ref[...] = acc.astype(o_ref.dtype)


def _matmul_cat(a1, a2, b, out_dtype, tm=1024, tn=512, name="matmul_cat"):
    m, kh = a1.shape
    n = b.shape[1]
    tm, tn = _pick(m, tm), _pick(n, tn)
    return pl.pallas_call(
        _mm2_kernel,
        grid=(m // tm, n // tn),
        in_specs=[pl.BlockSpec((tm, kh), lambda i, j: (i, 0)), pl.BlockSpec((tm, kh), lambda i, j: (i, 0)),
                  pl.BlockSpec((kh, tn), lambda i, j: (0, j)), pl.BlockSpec((kh, tn), lambda i, j: (1, j))],
        out_specs=pl.BlockSpec((tm, tn), lambda i, j: (i, j)),
        out_shape=jax.ShapeDtypeStruct((m, n), out_dtype),
        compiler_params=_cparams(("parallel", "arbitrary")),
        name=name,
    )(a1, a2, b, b)


def _sigmoid(x):
    return 1.0 / (1.0 + jnp.exp(-x))


def _swiglu_kernel(a_ref, wg_ref, wu_ref, o_ref):
    a = a_ref[...]
    g = jnp.dot(a, wg_ref[...], preferred_element_type=F32)
    u = jnp.dot(a, wu_ref[...], preferred_element_type=F32)
    o_ref[...] = (g * _sigmoid(g) * u).astype(o_ref.dtype)


def _swiglu(a, wg, wu, tm=1024, tn=512, name="swiglu"):
    m, kdim = a.shape
    f = wg.shape[1]
    tm, tn = _pick(m, tm), _pick(f, tn)
    wspec = pl.BlockSpec((kdim, tn), lambda i, j: (0, j))
    return pl.pallas_call(
        _swiglu_kernel,
        grid=(m // tm, f // tn),
        in_specs=[pl.BlockSpec((tm, kdim), lambda i, j: (i, 0)), wspec, wspec],
        out_specs=pl.BlockSpec((tm, tn), lambda i, j: (i, j)),
        out_shape=jax.ShapeDtypeStruct((m, f), BF16),
        compiler_params=_cparams(("parallel", "arbitrary")),
        name=name,
    )(a, wg, wu)


def _ln_kernel(*refs, alpha, has_res):
    if has_res:
        x_ref, r_ref, g_ref, b_ref, of_ref, ob_ref = refs
        x = alpha * x_ref[...] + r_ref[...]
    else:
        x_ref, g_ref, b_ref, of_ref, ob_ref = refs
        x = x_ref[...]
    mu = jnp.mean(x, axis=-1, keepdims=True)
    xc = x - mu
    var = jnp.mean(xc * xc, axis=-1, keepdims=True)
    y = xc * lax.rsqrt(var + LN_EPS) * g_ref[...] + b_ref[...]
    of_ref[...] = y
    ob_ref[...] = y.astype(BF16)


def _layer_norm(x, res, g, b, alpha=1.0, tm=256):
    m, d = x.shape
    tm = _pick(m, tm)
    row = pl.BlockSpec((tm, d), lambda i: (i, 0))
    vec = pl.BlockSpec((1, d), lambda i: (0, 0))
    args = [x] + ([res] if res is not None else []) + [g.reshape(1, d), b.reshape(1, d)]
    return pl.pallas_call(
        functools.partial(_ln_kernel, alpha=alpha, has_res=res is not None),
        grid=(m // tm,),
        in_specs=[row] * (len(args) - 2) + [vec, vec],
        out_specs=[row, row],
        out_shape=[jax.ShapeDtypeStruct((m, d), F32), jax.ShapeDtypeStruct((m, d), BF16)],
        compiler_params=_cparams(("parallel",)),
        name="layer_norm",
    )(*args)


def _router_kernel(x_ref, r_ref, o_ref, *, n_experts):
    logits = jnp.dot(x_ref[...], r_ref[...], precision=lax.Precision.HIGHEST, preferred_element_type=F32)
    lane = lax.broadcasted_iota(jnp.int32, logits.shape, 1).astype(F32)
    neg = -jnp.inf
    lg = jnp.where(lane < n_experts, logits, neg)
    m1 = jnp.max(lg, axis=1, keepdims=True)
    i1 = jnp.min(jnp.where(lg == m1, lane, float(LANE)), axis=1, keepdims=True)
    lg2 = jnp.where(lane == i1, neg, lg)
    m2 = jnp.max(lg2, axis=1, keepdims=True)
    i2 = jnp.min(jnp.where(lg2 == m2, lane, float(LANE)), axis=1, keepdims=True)
    e2 = jnp.exp(m2 - m1)
    w1 = 1.0 / (1.0 + e2)
    o_ref[...] = (jnp.where(lane == 0.0, i1, 0.0) + jnp.where(lane == 1.0, i2, 0.0)
                  + jnp.where(lane == 2.0, w1, 0.0) + jnp.where(lane == 3.0, e2 * w1, 0.0))


def _router_top2(x, router, tm=512):
    m, d = x.shape
    ne = router.shape[1]
    tm = _pick(m, tm)
    rpad = jnp.zeros((d, LANE), F32).at[:, :ne].set(router.astype(F32))
    return pl.pallas_call(
        functools.partial(_router_kernel, n_experts=ne),
        grid=(m // tm,),
        in_specs=[pl.BlockSpec((tm, d), lambda i: (i, 0)), pl.BlockSpec((d, LANE), lambda i: (0, 0))],
        out_specs=pl.BlockSpec((tm, LANE), lambda i: (i, 0)),
        out_shape=jax.ShapeDtypeStruct((m, LANE), F32),
        compiler_params=_cparams(("parallel",)),
        name="router",
    )(x, rpad)


MOE_TILE = 512
MOE_TOKENS = 256
TOP_K = 2


def _route_plan(top2, n_experts, tile):
    m = top2.shape[0]
    ef = top2[:, 0:TOP_K].astype(jnp.int32).reshape(-1)
    oh = (ef[:, None] == jnp.arange(n_experts, dtype=jnp.int32)[None, :]).astype(jnp.int32)
    rank = jnp.cumsum(oh, axis=0) - oh
    counts = jnp.sum(oh, axis=0)
    padded = ((counts + tile - 1) // tile) * tile
    ends = jnp.cumsum(padded)
    starts = ends - padded
    dest = jnp.sum(oh * (starts[None, :] + rank), axis=1).astype(jnp.int32)
    n_tiles = (TOP_K * m) // tile + n_experts
    tile_start = jnp.arange(n_tiles, dtype=jnp.int32) * tile
    tile_expert = jnp.sum((tile_start[:, None] >= ends[None, :]).astype(jnp.int32), axis=1)
    tile_expert = jnp.minimum(tile_expert, n_experts - 1).astype(jnp.int32)
    n_used = (ends[-1] // tile).astype(jnp.int32).reshape(1)
    return dest, tile_expert, n_used, n_tiles


def _dispatch_kernel(dest_ref, x_ref, _xg_in, xg_ref, sem, *, tq):
    base = pl.program_id(0) * tq

    def row_copy(r, slot):
        row = dest_ref[TOP_K * (base + r) + slot]
        return pltpu.make_async_copy(x_ref.at[pl.ds(r, 1)], xg_ref.at[pl.ds(row, 1)], sem)

    def issue(r, carry):
        for slot in range(TOP_K):
            row_copy(r, slot).start()
        return carry

    lax.fori_loop(0, tq, issue, 0, unroll=4)
    for _ in range(TOP_K):
        pltpu.make_async_copy(x_ref, xg_ref.at[pl.ds(0, tq)], sem).wait()


def _dispatch(xf, dest, n_rows, tq=MOE_TOKENS):
    m, d = xf.shape
    tq = _pick(m, tq)
    return pl.pallas_call(
        functools.partial(_dispatch_kernel, tq=tq),
        grid_spec=pltpu.PrefetchScalarGridSpec(
            num_scalar_prefetch=1,
            grid=(m // tq,),
            in_specs=[pl.BlockSpec((tq, d), lambda i, dest: (i, 0)), pl.BlockSpec(memory_space=pl.ANY)],
            out_specs=pl.BlockSpec(memory_space=pl.ANY),
            scratch_shapes=[pltpu.SemaphoreType.DMA(())],
        ),
        out_shape=jax.ShapeDtypeStruct((n_rows, d), F32),
        input_output_aliases={2: 0},
        compiler_params=_cparams(("arbitrary",)),
        name="moe_dispatch",
    )(dest, xf, jnp.zeros((n_rows, d), F32))


def _gswiglu_kernel(te_ref, nu_ref, a_ref, wg_ref, wu_ref, o_ref, abf_ref):
    i = pl.program_id(0)

    @pl.when(pl.program_id(1) == 0)
    def _():
        abf_ref[...] = a_ref[...].astype(BF16)

    @pl.when(i < nu_ref[0])
    def _():
        a = abf_ref[...]
        g = jnp.dot(a, wg_ref[...], preferred_element_type=F32)
        u = jnp.dot(a, wu_ref[...], preferred_element_type=F32)
        o_ref[...] = (g * _sigmoid(g) * u).astype(o_ref.dtype)

    @pl.when(i >= nu_ref[0])
    def _():
        o_ref[...] = jnp.zeros_like(o_ref)


def _gdown_kernel(te_ref, nu_ref, a_ref, w_ref, o_ref):
    i = pl.program_id(0)

    @pl.when(i < nu_ref[0])
    def _():
        o_ref[...] = jnp.dot(a_ref[...], w_ref[...], preferred_element_type=F32)

    @pl.when(i >= nu_ref[0])
    def _():
        o_ref[...] = jnp.zeros_like(o_ref)


def _grouped_experts(xg, wg, wu, wd, tile_expert, n_used, tile=MOE_TILE, tn=512, tn_down=1024):
    p, d = xg.shape
    f = wg.shape[2]
    tn, tn_down = _pick(f, tn), _pick(d, tn_down)
    nt = p // tile

    def wspec(rows, cols):
        return pl.BlockSpec((None, rows, cols),
                            lambda i, j, te, nu: (te[i], 0, jnp.where(i < nu[0], j, 0)))

    hdn = pl.pallas_call(
        _gswiglu_kernel,
        grid_spec=pltpu.PrefetchScalarGridSpec(
            num_scalar_prefetch=2,
            grid=(nt, f // tn),
            in_specs=[pl.BlockSpec((tile, d), lambda i, j, te, nu: (i, 0)), wspec(d, tn), wspec(d, tn)],
            out_specs=pl.BlockSpec((tile, tn), lambda i, j, te, nu: (i, j)),
            scratch_shapes=[pltpu.VMEM((tile, d), BF16)],
        ),
        out_shape=jax.ShapeDtypeStruct((p, f), BF16),
        compiler_params=_cparams(("arbitrary", "arbitrary")),
        name="moe_swiglu",
    )(tile_expert, n_used, xg, wg, wu)
    return pl.pallas_call(
        _gdown_kernel,
        grid_spec=pltpu.PrefetchScalarGridSpec(
            num_scalar_prefetch=2,
            grid=(nt, d // tn_down),
            in_specs=[pl.BlockSpec((tile, f), lambda i, j, te, nu: (i, 0)), wspec(f, tn_down)],
            out_specs=pl.BlockSpec((tile, tn_down), lambda i, j, te, nu: (i, j)),
        ),
        out_shape=jax.ShapeDtypeStruct((p, d), F32),
        compiler_params=_cparams(("arbitrary", "arbitrary")),
        name="moe_down",
    )(tile_expert, n_used, hdn, wd)


def _combine_ln_kernel(dest_ref, x_ref, top2_ref, g_ref, b_ref, yg_ref, of_ref, ob_ref, buf_ref, sem, *, alpha, tq):
    base = pl.program_id(0) * tq

    def row_copy(r, slot):
        row = dest_ref[TOP_K * (base + r) + slot]
        return pltpu.make_async_copy(yg_ref.at[pl.ds(row, 1)], buf_ref.at[slot, pl.ds(r, 1)], sem)

    def issue(r, carry):
        for slot in range(TOP_K):
            row_copy(r, slot).start()
        return carry

    lax.fori_loop(0, tq, issue, 0, unroll=4)
    for slot in range(TOP_K):
        pltpu.make_async_copy(yg_ref.at[pl.ds(0, tq)], buf_ref.at[slot], sem).wait()
    top2 = top2_ref[...]
    ffn = top2[:, TOP_K:TOP_K + 1] * buf_ref[0]
    for slot in range(1, TOP_K):
        ffn = ffn + top2[:, TOP_K + slot:TOP_K + slot + 1] * buf_ref[slot]
    x = alpha * x_ref[...] + ffn
    mu = jnp.mean(x, axis=-1, keepdims=True)
    xc = x - mu
    var = jnp.mean(xc * xc, axis=-1, keepdims=True)
    y = xc * lax.rsqrt(var + LN_EPS) * g_ref[...] + b_ref[...]
    of_ref[...] = y
    ob_ref[...] = y.astype(BF16)


def _combine_layer_norm(x, yg, top2, dest, g, b, alpha, tq=MOE_TOKENS):
    m, d = x.shape
    tq = _pick(m, tq)
    row = pl.BlockSpec((tq, d), lambda i, dest: (i, 0))
    vec = pl.BlockSpec((1, d), lambda i, dest: (0, 0))
    return pl.pallas_call(
        functools.partial(_combine_ln_kernel, alpha=alpha, tq=tq),
        grid_spec=pltpu.PrefetchScalarGridSpec(
            num_scalar_prefetch=1,
            grid=(m // tq,),
            in_specs=[row, pl.BlockSpec((tq, LANE), lambda i, dest: (i, 0)), vec, vec,
                      pl.BlockSpec(memory_space=pl.ANY)],
            out_specs=[row, row],
            scratch_shapes=[pltpu.VMEM((TOP_K, tq, d), F32), pltpu.SemaphoreType.DMA(())],
        ),
        out_shape=[jax.ShapeDtypeStruct((m, d), F32), jax.ShapeDtypeStruct((m, d), BF16)],
        compiler_params=_cparams(("arbitrary",)),
        name="moe_combine_ln",
    )(dest, x, top2, g.reshape(1, d), b.reshape(1, d), yg)


def _dot(a, b):
    return jnp.dot(a, b, preferred_element_type=F32)


def _dot_nt(a, b):
    return lax.dot_general(a, b, (((1,), (1,)), ((), ())), preferred_element_type=F32)


def _dot_tn(a, b):
    return lax.dot_general(a, b, (((0,), (0,)), ((), ())), preferred_element_type=F32)


def _split2(x):
    hi = x.astype(BF16)
    return hi, (x - hi.astype(F32)).astype(BF16)


def _dot_split_lhs(stat2, x):
    hi, lo = _split2(x)
    return _dot(stat2, jnp.concatenate([hi, lo], axis=0))


def _dot_split_rhs(x, stat):
    hi, lo = _split2(x)
    return _dot(hi, stat) + _dot(lo, stat)


def _log_sigmoid(z):
    return jnp.minimum(z, 0.0) - jnp.log(1.0 + jnp.exp(-jnp.abs(z)))


def _softplus(z):
    return jnp.maximum(z, 0.0) + jnp.log(1.0 + jnp.exp(-jnp.abs(z)))


def _n_levels(c):
    return int(round(math.log2(c)))


def _group_rows(gi, ngroups, unroll, rev):
    gg = (ngroups - 1 - gi) if rev else gi
    order = range(unroll - 1, -1, -1) if rev else range(unroll)
    return [pl.ds(pl.multiple_of((gg * unroll + u) * CHUNK, CHUNK), CHUNK) for u in order]


@functools.lru_cache(maxsize=None)
def _hg_static(c, rev):
    n = _n_levels(c)
    t = np.arange(c)[:, None]
    r = np.arange(c)[None, :]
    mats = [r <= t]
    d_mats, e_mats, masks = [], [], [t == r]
    for lvl in range(1, n + 1):
        blk, half = 1 << lvl, 1 << (lvl - 1)
        b0 = (t // blk) * blk
        mid = b0 + half - 1
        second = (t - b0) >= half
        d_mats.append(second & (r > mid) & (r <= t))
        e_mats.append((~second) & (r > t) & (r <= mid))
        masks.append((b0 == (r // blk) * blk) & second & ((r % blk) < half))
    mats = mats + d_mats + e_mats + [r > t]
    if rev:
        mats = [m[::-1, ::-1] for m in mats]
        masks = [m[::-1, ::-1] for m in masks]
    stat = np.concatenate(mats, axis=0).astype(np.float32)
    return stat, np.stack(masks).astype(np.float32)


def _hgrn2_kernel(*refs, rev, nchunk, unroll, epilogue, scale):
    if epilogue:
        (q_ref, z_ref, v_ref, llb_ref, l1m_ref, stat_ref, mask_ref,
         of_ref, og_ref, ng_ref, o_ref, st_ref) = refs
    else:
        q_ref, z_ref, v_ref, llb_ref, l1m_ref, stat_ref, mask_ref, o_ref, st_ref = refs
    c = CHUNK
    n = _n_levels(c)
    ngroups = nchunk // unroll

    @pl.when(pl.program_id(2) == 0)
    def _():
        st_ref[...] = jnp.zeros_like(st_ref)

    log_lb = llb_ref[...]
    log_1m_lb = l1m_ref[...]

    def group_step(gi, carry):
        rows_list = _group_rows(gi, ngroups, unroll, rev)
        nu = len(rows_list)
        q = [q_ref[rows, :] * scale for rows in rows_list]
        vb = [v_ref[rows, :].astype(BF16) for rows in rows_list]
        b = [log_1m_lb + _log_sigmoid(z_ref[rows, :]) for rows in rows_list]
        g = [jnp.maximum(log_lb, bi) + jnp.log(1.0 + jnp.exp(-jnp.abs(log_lb - bi))) for bi in b]
        k = [1.0 - jnp.exp(gi_) for gi_ in g]
        xall = _dot_split_lhs(stat_ref[...], jnp.concatenate(g, axis=1))

        def expo(blk, i):
            return xall[blk * c:(blk + 1) * c, i * LANE:(i + 1) * LANE]

        att = [mask_ref[0] * _dot_nt(q[i].astype(BF16), k[i].astype(BF16)) for i in range(nu)]
        for lvl in range(1, n + 1):
            ql = [(q[i] * jnp.exp(expo(lvl, i))).astype(BF16) for i in range(nu)]
            kl = [(k[i] * jnp.exp(expo(n + lvl, i))).astype(BF16) for i in range(nu)]
            att = [att[i] + mask_ref[lvl] * _dot_nt(ql[i], kl[i]) for i in range(nu)]
        o_intra = [_dot(att[i].astype(BF16), vb[i]) for i in range(nu)]
        qd = [(q[i] * jnp.exp(expo(0, i))).astype(BF16) for i in range(nu)]
        kd = [(k[i] * jnp.exp(expo(2 * n + 1, i))).astype(BF16) for i in range(nu)]
        dst = [_dot_tn(vb[i], kd[i]) for i in range(nu)]
        edge_row = 0 if rev else c - 1
        de = [jnp.exp(expo(0, i)[edge_row:edge_row + 1]) for i in range(nu)]
        states = [st_ref[...]]
        for i in range(nu):
            states.append(states[i] * de[i] + dst[i])
        st_ref[...] = states[nu]
        outs = [o_intra[i] + _dot_nt(qd[i], states[i].astype(BF16)) for i in range(nu)]
        for rows, o in zip(rows_list, outs):
            if epilogue:
                tot = o + of_ref[rows, :]
                y = tot * lax.rsqrt(jnp.mean(tot * tot, axis=-1, keepdims=True) + RMS_EPS) * ng_ref[...]
                gate = og_ref[rows, :]
                o_ref[rows, :] = (y * gate * _sigmoid(gate)).astype(o_ref.dtype)
            else:
                o_ref[rows, :] = o
        return carry

    lax.fori_loop(0, ngroups, group_step, 0)


def _hgrn2(p3, o_fwd, llb, l1m, norm_g, *, rev, heads, col, d_model, rows=512, unroll=8):
    bsz, s, _ = p3.shape
    rows = _pick(s, rows)
    ns = s // rows
    nchunk = rows // CHUNK
    unroll = math.gcd(unroll, nchunk)
    stat, masks = _hg_static(CHUNK, rev)
    stat = jnp.asarray(np.concatenate([stat, stat], axis=1), BF16)
    masks = jnp.asarray(masks, F32)
    zcol = col["hg_f_bwd"] if rev else col["hg_f_fwd"]
    seq = (lambda i: ns - 1 - i) if rev else (lambda i: i)

    def pspec(c0):
        return pl.BlockSpec((None, rows, LANE), lambda b, h, i: (b, seq(i), c0 + h))

    vec = pl.BlockSpec((None, 1, LANE), lambda b, h, i: (h, 0, 0))
    in_specs = [pspec(col["hg_q"]), pspec(zcol), pspec(col["hg_i"]), vec, vec,
                pl.BlockSpec(stat.shape, lambda b, h, i: (0, 0)),
                pl.BlockSpec(masks.shape, lambda b, h, i: (0, 0, 0))]
    args = [p3, p3, p3, llb, l1m, stat, masks]
    if rev:
        in_specs += [pl.BlockSpec((None, rows, LANE), lambda b, h, i: (b, seq(i), h)),
                     pspec(col["hg_g"]),
                     pl.BlockSpec((1, LANE), lambda b, h, i: (0, 0))]
        args += [o_fwd, p3, norm_g]
        out_shape = jax.ShapeDtypeStruct((bsz, s, heads * LANE), BF16)
    else:
        out_shape = jax.ShapeDtypeStruct((bsz, s, heads * LANE), F32)
    return pl.pallas_call(
        functools.partial(_hgrn2_kernel, rev=rev, nchunk=nchunk, unroll=unroll, epilogue=rev,
                          scale=HEAD_DIM ** -0.5),
        grid=(bsz, heads, ns),
        in_specs=in_specs,
        out_specs=pl.BlockSpec((None, rows, LANE), lambda b, h, i: (b, seq(i), h)),
        out_shape=out_shape,
        scratch_shapes=[pltpu.VMEM((HEAD_DIM, HEAD_DIM), F32)],
        compiler_params=_cparams(("parallel", "parallel", "arbitrary")),
        name="hgrn2_bwd" if rev else "hgrn2_fwd",
    )(*args)


def _gd_prep_kernel(x_ref, w_ref, o_ref, xp_ref, *, s, tile, n_qk, n_q, scale_q):
    j = pl.program_id(1)
    halo = 8
    xp_ref[0:halo, :] = jnp.zeros((halo, LANE), F32)
    xp_ref[halo + s:2 * halo + s, :] = jnp.zeros((halo, LANE), F32)
    xp_ref[halo:halo + s, :] = x_ref[...]
    w = w_ref[...]
    inv_scale = jnp.where(j < n_q, scale_q, 1.0)

    def tile_step(ti, carry):
        r0 = pl.multiple_of(ti * tile, tile)
        win = xp_ref[pl.ds(r0, tile + 2 * halo), :]
        acc = jnp.zeros((tile, LANE), F32)
        for tap in range(CONV_K):
            off = halo + tap - CONV_K // 2
            acc = acc + w[tap:tap + 1, :] * win[off:off + tile, :]
        y = acc * _sigmoid(acc)
        inv = lax.rsqrt(jnp.sum(y * y, axis=-1, keepdims=True) + RMS_EPS) * inv_scale
        o_ref[pl.ds(r0, tile), :] = y * jnp.where(j < n_qk, inv, 1.0)
        return carry

    lax.fori_loop(0, s // tile, tile_step, 0)


def _gd_prep(p3, conv_w, *, col0, heads, tile=1024):
    bsz, s, _ = p3.shape
    tile = _pick(s, tile)
    nblk = 3 * heads
    return pl.pallas_call(
        functools.partial(_gd_prep_kernel, s=s, tile=tile, n_qk=2 * heads, n_q=heads, scale_q=HEAD_DIM ** -0.5),
        grid=(bsz, nblk),
        in_specs=[pl.BlockSpec((None, s, LANE), lambda b, j: (b, 0, col0 + j)),
                  pl.BlockSpec((CONV_K, LANE), lambda b, j: (0, j))],
        out_specs=pl.BlockSpec((None, s, LANE), lambda b, j: (b, 0, j)),
        out_shape=jax.ShapeDtypeStruct((bsz, s, nblk * LANE), F32),
        scratch_shapes=[pltpu.VMEM((s + 16, LANE), F32)],
        compiler_params=_cparams(("parallel", "parallel")),
        name="gd_prep",
    )(p3, conv_w)


@functools.lru_cache(maxsize=None)
def _gd_static(c, rev):
    t = np.arange(c)[:, None]
    s = np.arange(c)[None, :]
    incl = (s >= t) if rev else (s <= t)
    strict = (s > t) if rev else (s < t)
    mats = [incl, strict, t == s, np.ones((c, c), bool), strict & ((t // 8) == (s // 8))]
    blk = 16
    while blk <= c:
        half = blk // 2
        same = (t // blk) == (s // blk)
        t2, s2 = (t % blk) >= half, (s % blk) >= half
        mats.append(same & ((~t2) & s2 if rev else t2 & (~s2)))
        blk *= 2
    return np.stack(mats).astype(np.float32)


def _gdn_kernel(*refs, rev, nchunk, unroll, epilogue, heads, hb):
    if epilogue:
        (q_ref, k_ref, v_ref, pg_ref, na_ref, dtb_ref, stat_ref, stat2_ref,
         of_ref, og_ref, ng_ref, o_ref, s_ref, gb_ref, bb_ref) = refs
    else:
        q_ref, k_ref, v_ref, pg_ref, na_ref, dtb_ref, stat_ref, stat2_ref, o_ref, s_ref, gb_ref, bb_ref = refs
    c = CHUNK
    ngroups = nchunk // unroll
    hg = pl.program_id(1)

    @pl.when(pl.program_id(2) == 0)
    def _():
        s_ref[...] = jnp.zeros_like(s_ref)

    incl = stat_ref[0]
    strict = stat_ref[1]
    eye = stat_ref[2]
    incl2 = stat2_ref[0]
    ones2 = stat2_ref[1]
    n_lvl = stat_ref.shape[0] - 5

    pg = pg_ref[...]
    glog = na_ref[...] * _softplus(pg + dtb_ref[...])
    beta = _sigmoid(pg)
    cum_all = jnp.concatenate(
        [_dot_split_lhs(incl2, glog[ci * c:(ci + 1) * c]) for ci in range(nchunk)], axis=0)
    rsel = lax.broadcasted_iota(jnp.int32, (LANE, LANE), 0)
    d = 1 if rev else 0
    for hh in range(hb):
        h = hg * hb + hh
        gb_ref[hh] = _dot_split_rhs(cum_all, jnp.where(rsel == d * heads + h, 1.0, 0.0).astype(BF16))
        bb_ref[hh] = _dot(beta.astype(BF16), jnp.where(rsel == (2 + d) * heads + h, 1.0, 0.0).astype(BF16))

    d8 = stat_ref[4]
    edge_row = 0 if rev else c - 1

    def group_step(gi, carry):
        rows_list = _group_rows(gi, ngroups, unroll, rev)
        items = [(rows, hh) for rows in rows_list for hh in range(hb)]
        ni = len(items)
        rng = range(ni)
        lanes = [slice(hh * LANE, (hh + 1) * LANE) for _, hh in items]
        q = [q_ref[rows, lanes[i]] for i, (rows, _) in enumerate(items)]
        k = [k_ref[rows, lanes[i]] for i, (rows, _) in enumerate(items)]
        v = [v_ref[rows, lanes[i]] for i, (rows, _) in enumerate(items)]
        cum = [gb_ref[hh, rows, :] for rows, hh in items]
        bb = [bb_ref[hh, rows, :] for rows, hh in items]
        cum_row = [_dot_split_lhs(ones2, cum[i][:, 0:c] * eye) for i in rng]
        decay = [jnp.exp(jnp.where(incl > 0, cum[i][:, 0:c] - cum_row[i], -jnp.inf)) for i in rng]
        ecum = [jnp.exp(cum[i]) for i in rng]
        kb = [k[i] * bb[i] for i in rng]
        kbf = [k[i].astype(BF16) for i in rng]
        a = [strict * _dot_nt(kb[i].astype(BF16), kbf[i]) * decay[i] for i in rng]
        qk = [(_dot_nt(q[i].astype(BF16), kbf[i]) * decay[i]).astype(BF16) for i in rng]
        a0 = [a[i] * d8 for i in rng]
        a0b = [a0[i].astype(BF16) for i in rng]
        a2b = [_dot(a0b[i], a0b[i]).astype(BF16) for i in rng]
        xm = [eye - a0[i] for i in rng]
        ym = [xm[i] + _dot(xm[i].astype(BF16), a2b[i]) for i in rng]
        a4b = [_dot(a2b[i], a2b[i]).astype(BF16) for i in rng]
        tm = [ym[i] + _dot(ym[i].astype(BF16), a4b[i]) for i in rng]
        for lvl in range(n_lvl):
            off = stat_ref[5 + lvl]
            tmb = [tm[i].astype(BF16) for i in rng]
            mid = [_dot(tmb[i], (a[i] * off).astype(BF16)).astype(BF16) for i in rng]
            tm = [tm[i] - _dot(mid[i], tmb[i]) for i in rng]
        tmb = [tm[i].astype(BF16) for i in rng]
        u = [_dot(tmb[i], (v[i] * bb[i]).astype(BF16)) for i in rng]
        w = [_dot(tmb[i], (kb[i] * ecum[i]).astype(BF16)).astype(BF16) for i in rng]
        edge = [cum[i][edge_row:edge_row + 1] for i in rng]
        qd = [(q[i] * ecum[i]).astype(BF16) for i in rng]
        kd = [(k[i] * jnp.exp(edge[i] - cum[i])).astype(BF16) for i in rng]
        de = [jnp.exp(edge[i]) for i in rng]
        st = [s_ref[hh] for hh in range(hb)]
        heads_rng = range(hb)
        for ui, rows in enumerate(rows_list):
            idx = [ui * hb + hh for hh in heads_rng]
            stb = [st[hh].astype(BF16) for hh in heads_rng]
            ws = [_dot(w[idx[hh]], stb[hh]) for hh in heads_rng]
            oq = [_dot(qd[idx[hh]], stb[hh]) for hh in heads_rng]
            vnb = [(u[idx[hh]] - ws[hh]).astype(BF16) for hh in heads_rng]
            dst = [_dot_tn(kd[idx[hh]], vnb[hh]) for hh in heads_rng]
            o = [oq[hh] + _dot(qk[idx[hh]], vnb[hh]) for hh in heads_rng]
            st = [st[hh] * de[idx[hh]] + dst[hh] for hh in heads_rng]
            for hh in heads_rng:
                ln = slice(hh * LANE, (hh + 1) * LANE)
                if epilogue:
                    tot = o[hh] + of_ref[rows, ln]
                    y = tot * lax.rsqrt(jnp.mean(tot * tot, axis=-1, keepdims=True) + RMS_EPS) * ng_ref[...]
                    gate = og_ref[rows, ln]
                    o_ref[rows, ln] = (y * gate * _sigmoid(gate)).astype(o_ref.dtype)
                else:
                    o_ref[rows, ln] = o[hh]
        for hh in heads_rng:
            s_ref[hh] = st[hh]
        return carry

    lax.fori_loop(0, ngroups, group_step, 0)


def _gdn(qkv, pg, p3, o_fwd, neg_a, dtb, norm_g, *, rev, heads, gcol, rows=512, unroll=4, hb=8):
    bsz, s, _ = qkv.shape
    rows = _pick(s, rows)
    ns = s // rows
    nchunk = rows // CHUNK
    unroll = math.gcd(unroll, nchunk)
    hb = math.gcd(hb, heads)
    stat_np = _gd_static(CHUNK, rev)
    stat = jnp.asarray(stat_np, F32)
    stat2 = jnp.asarray(np.stack([np.concatenate([stat_np[i], stat_np[i]], axis=1) for i in (0, 3)]), BF16)
    seq = (lambda i: ns - 1 - i) if rev else (lambda i: i)
    wide = hb * LANE

    def qspec(c0):
        return pl.BlockSpec((None, rows, wide), lambda b, h, i: (b, seq(i), c0 // hb + h))

    vec = pl.BlockSpec((1, LANE), lambda b, h, i: (0, 0))
    in_specs = [qspec(0), qspec(heads), qspec(2 * heads),
                pl.BlockSpec((None, rows, LANE), lambda b, h, i: (b, seq(i), 0)),
                vec, vec, pl.BlockSpec(stat.shape, lambda b, h, i: (0, 0, 0)),
                pl.BlockSpec(stat2.shape, lambda b, h, i: (0, 0, 0))]
    args = [qkv, qkv, qkv, pg, neg_a, dtb, stat, stat2]
    if rev:
        in_specs += [qspec(0), qspec(gcol), vec]
        args += [o_fwd, p3, norm_g]
    out_shape = jax.ShapeDtypeStruct((bsz, s, heads * LANE), BF16 if rev else F32)
    out_spec = qspec(0)
    return pl.pallas_call(
        functools.partial(_gdn_kernel, rev=rev, nchunk=nchunk, unroll=unroll, epilogue=rev, heads=heads, hb=hb),
        grid=(bsz, heads // hb, ns),
        in_specs=in_specs,
        out_specs=out_spec,
        out_shape=out_shape,
        scratch_shapes=[pltpu.VMEM((hb, HEAD_DIM, HEAD_DIM), F32),
                        pltpu.VMEM((hb, rows, LANE), F32), pltpu.VMEM((hb, rows, LANE), F32)],
        compiler_params=_cparams(("parallel", "parallel", "arbitrary")),
        name="gdn_bwd" if rev else "gdn_fwd",
    )(*args)


def _pad_cols(w, n):
    return jnp.pad(w, ((0, 0),) * (w.ndim - 1) + ((0, n - w.shape[-1]),))


def kernel(x, ln_in_g, ln_in_b, w_in, hg_lb_param, gd_conv_w, gd_a_log, gd_dt_bias, hg_norm_g, gd_norm_g, w_out,
           ln1_g, ln1_b, ffn_w_gate, ffn_w_up, ffn_w_down, moe_router, moe_w_gate, moe_w_up, moe_w_down,
           ln2_g, ln2_b):
    bsz, s, d = x.shape
    depth = w_in.shape[0]
    m = bsz * s
    half = d // 2
    heads = half // HEAD_DIM
    alpha = (2 * depth) ** 0.25
    n_main = 8 * half
    n_gate = N_GATE_COLS * heads
    names = ("hg_q", "hg_f_fwd", "hg_f_bwd", "hg_i", "hg_g", "gd_q", "gd_k", "gd_v", "gd_g")
    col = {nm: i * heads for i, nm in enumerate(names)}

    lb = jnp.cumsum(jax.nn.softmax(hg_lb_param.astype(F32), axis=0), axis=0)
    lb = lb - lb[0:1]
    log_lb = jnp.log(lb).reshape(depth, 2, heads, 1, LANE)
    log_1m_lb = jnp.log1p(-lb).reshape(depth, 2, heads, 1, LANE)

    xf, xb = _layer_norm(x.reshape(m, d), None, ln_in_g, ln_in_b)
    for l in range(depth):
        w_l = w_in[l]
        w_main = jnp.concatenate([w_l[:, :n_main], w_l[:, n_main + n_gate:]], axis=1).astype(BF16)
        w_gate = _pad_cols(w_l[:, n_main:n_main + n_gate], LANE).astype(BF16)
        p3 = _matmul(xb, w_main, F32, tn=1024, name="in_proj").reshape(bsz, s, -1)
        pg = _matmul(xb, w_gate, F32, tn=LANE, name="gate_proj").reshape(bsz, s, LANE)

        o_f = _hgrn2(p3, None, log_lb[l, 0], log_1m_lb[l, 0], None, rev=False, heads=heads, col=col, d_model=d)
        hg_out = _hgrn2(p3, o_f, log_lb[l, 1], log_1m_lb[l, 1], hg_norm_g[l].reshape(1, LANE).astype(F32),
                        rev=True, heads=heads, col=col, d_model=d)

        qkv = _gd_prep(p3, gd_conv_w[l].astype(F32), col0=col["gd_q"], heads=heads)
        neg_a = jnp.zeros((1, LANE), F32).at[0, :2 * heads].set(-jnp.exp(gd_a_log[l].astype(F32)).reshape(-1))
        dtb = jnp.zeros((1, LANE), F32).at[0, :2 * heads].set(gd_dt_bias[l].astype(F32).reshape(-1))
        gn = gd_norm_g[l].reshape(1, LANE).astype(F32)
        g_f = _gdn(qkv, pg, None, None, neg_a, dtb, None, rev=False, heads=heads, gcol=col["gd_g"])
        gd_out = _gdn(qkv, pg, p3, g_f, neg_a, dtb, gn, rev=True, heads=heads, gcol=col["gd_g"])

        mix = _matmul_cat(hg_out.reshape(m, half), gd_out.reshape(m, half), w_out[l].astype(BF16), F32,
                          name="out_proj")
        xf, xb = _layer_norm(xf, mix, ln1_g[l], ln1_b[l], alpha=alpha)

        j = l // 2
        if l % 2 == 0:
            f = ffn_w_gate.shape[-1]
            fp = -(-f // 1024) * 1024
            hdn = _swiglu(xb, _pad_cols(ffn_w_gate[j], fp).astype(BF16), _pad_cols(ffn_w_up[j], fp).astype(BF16))
            wd = jnp.pad(ffn_w_down[j], ((0, fp - f), (0, 0))).astype(BF16)
            ffn = _matmul(hdn, wd, F32, tn=1024, tk=2816, name="down_proj")
            xf, xb = _layer_norm(xf, ffn, ln2_g[l], ln2_b[l], alpha=alpha)
        else:
            n_experts = moe_router.shape[-1]
            top2 = _router_top2(xf, moe_router[j])
            dest, tile_expert, n_used, n_tiles = _route_plan(top2, n_experts, MOE_TILE)
            xg = _dispatch(xf, dest, n_tiles * MOE_TILE)
            yg = _grouped_experts(xg, moe_w_gate[j].astype(BF16), moe_w_up[j].astype(BF16),
                                  moe_w_down[j].astype(BF16), tile_expert, n_used)
            xf, xb = _combine_layer_norm(xf, yg, top2, dest, ln2_g[l], ln2_b[l], alpha)
    return xf.reshape(bsz, s, d)
```

```python
import functools
import math

import numpy as np
import jax
import jax.numpy as jnp
from jax import lax
from jax.experimental import pallas as pl
from jax.experimental.pallas import tpu as pltpu

F32 = jnp.float32
BF16 = jnp.bfloat16

LANE = 128
HEAD_DIM = 128
CHUNK = 64
CONV_K = 5
N_GATE_COLS = 4
LN_EPS = 1e-5
RMS_EPS = 1e-6
VMEM_LIMIT_BYTES = 56 * 1024 * 1024


def _cparams(sem):
    return pltpu.CompilerParams(dimension_semantics=sem, vmem_limit_bytes=VMEM_LIMIT_BYTES)


def _pick(n, pref):
    t = min(n, pref)
    while n % t:
        t -= LANE
    return t


def _mm_kernel(a_ref, b_ref, o_ref):
    o_ref[...] = jnp.dot(a_ref[...], b_ref[...], preferred_element_type=F32).astype(o_ref.dtype)


def _mm_acc_kernel(a_ref, b_ref, o_ref, acc_ref, *, nk):
    k = pl.program_id(2)
    prod = jnp.dot(a_ref[...], b_ref[...], preferred_element_type=F32)

    @pl.when(k == 0)
    def _():
        acc_ref[...] = prod

    @pl.when(k > 0)
    def _():
        acc_ref[...] += prod

    @pl.when(k == nk - 1)
    def _():
        o_ref[...] = acc_ref[...].astype(o_ref.dtype)


def _matmul(a, b, out_dtype, tm=1024, tn=512, tk=None, name="matmul"):
    m, kdim = a.shape
    n = b.shape[1]
    tm, tn = _pick(m, tm), _pick(n, tn)
    tk = kdim if tk is None else _pick(kdim, tk)
    nk = kdim // tk
    if nk == 1:
        return pl.pallas_call(
            _mm_kernel,
            grid=(m // tm, n // tn),
            in_specs=[pl.BlockSpec((tm, kdim), lambda i, j: (i, 0)),
                      pl.BlockSpec((kdim, tn), lambda i, j: (0, j))],
            out_specs=pl.BlockSpec((tm, tn), lambda i, j: (i, j)),
            out_shape=jax.ShapeDtypeStruct((m, n), out_dtype),
            compiler_params=_cparams(("parallel", "arbitrary")),
            name=name,
        )(a, b)
    return pl.pallas_call(
        functools.partial(_mm_acc_kernel, nk=nk),
        grid=(m // tm, n // tn, nk),
        in_specs=[pl.BlockSpec((tm, tk), lambda i, j, k: (i, k)),
                  pl.BlockSpec((tk, tn), lambda i, j, k: (k, j))],
        out_specs=pl.BlockSpec((tm, tn), lambda i, j, k: (i, j)),
        out_shape=jax.ShapeDtypeStruct((m, n), out_dtype),
        scratch_shapes=[pltpu.VMEM((tm, tn), F32)],
        compiler_params=_cparams(("parallel", "arbitrary", "arbitrary")),
        name=name,
    )(a, b)


def _mm2_kernel(a1_ref, a2_ref, b1_ref, b2_ref, o_ref):
    acc = jnp.dot(a1_ref[...], b1_ref[...], preferred_element_type=F32)
    acc = acc + jnp.dot(a2_ref[...], b2_ref[...], preferred_element_type=F32)
    o_ref[...] = acc.astype(o_ref.dtype)


def _matmul_cat(a1, a2, b, out_dtype, tm=1024, tn=512, name="matmul_cat"):
    m, kh = a1.shape
    n = b.shape[1]
    tm, tn = _pick(m, tm), _pick(n, tn)
    return pl.pallas_call(
        _mm2_kernel,
        grid=(m // tm, n // tn),
        in_specs=[pl.BlockSpec((tm, kh), lambda i, j: (i, 0)), pl.BlockSpec((tm, kh), lambda i, j: (i, 0)),
                  pl.BlockSpec((kh, tn), lambda i, j: (0, j)), pl.BlockSpec((kh, tn), lambda i, j: (1, j))],
        out_specs=pl.BlockSpec((tm, tn), lambda i, j: (i, j)),
        out_shape=jax.ShapeDtypeStruct((m, n), out_dtype),
        compiler_params=_cparams(("parallel", "arbitrary")),
        name=name,
    )(a1, a2, b, b)


def _sigmoid(x):
    return 1.0 / (1.0 + jnp.exp(-x))


def _dot(a, b):
    return jnp.dot(a, b, preferred_element_type=F32)


def _split2(x):
    hi = x.astype(BF16)
    return hi, (x - hi.astype(F32)).astype(BF16)


def _swiglu_kernel(a_ref, wg_ref, wu_ref, o_ref):
    a = a_ref[...]
    g = jnp.dot(a, wg_ref[...], preferred_element_type=F32)
    u = jnp.dot(a, wu_ref[...], preferred_element_type=F32)
    o_ref[...] = (g * _sigmoid(g) * u).astype(o_ref.dtype)


def _swiglu(a, wg, wu, tm=1024, tn=512, name="swiglu"):
    m, kdim = a.shape
    f = wg.shape[1]
    tm, tn = _pick(m, tm), _pick(f, tn)
    wspec = pl.BlockSpec((kdim, tn), lambda i, j: (0, j))
    return pl.pallas_call(
        _swiglu_kernel,
        grid=(m // tm, f // tn),
        in_specs=[pl.BlockSpec((tm, kdim), lambda i, j: (i, 0)), wspec, wspec],
        out_specs=pl.BlockSpec((tm, tn), lambda i, j: (i, j)),
        out_shape=jax.ShapeDtypeStruct((m, f), BF16),
        compiler_params=_cparams(("parallel", "arbitrary")),
        name=name,
    )(a, wg, wu)


def _ln_kernel(*refs, alpha, has_res):
    if has_res:
        x_ref, r_ref, g_ref, b_ref, of_ref, ob_ref = refs
        x = alpha * x_ref[...] + r_ref[...]
    else:
        x_ref, g_ref, b_ref, of_ref, ob_ref = refs
        x = x_ref[...]
    mu = jnp.mean(x, axis=-1, keepdims=True)
    xc = x - mu
    var = jnp.mean(xc * xc, axis=-1, keepdims=True)
    y = xc * lax.rsqrt(var + LN_EPS) * g_ref[...] + b_ref[...]
    of_ref[...] = y
    ob_ref[...] = y.astype(BF16)


def _layer_norm(x, res, g, b, alpha=1.0, tm=256):
    m, d = x.shape
    tm = _pick(m, tm)
    row = pl.BlockSpec((tm, d), lambda i: (i, 0))
    vec = pl.BlockSpec((1, d), lambda i: (0, 0))
    args = [x] + ([res] if res is not None else []) + [g.reshape(1, d), b.reshape(1, d)]
    return pl.pallas_call(
        functools.partial(_ln_kernel, alpha=alpha, has_res=res is not None),
        grid=(m // tm,),
        in_specs=[row] * (len(args) - 2) + [vec, vec],
        out_specs=[row, row],
        out_shape=[jax.ShapeDtypeStruct((m, d), F32), jax.ShapeDtypeStruct((m, d), BF16)],
        compiler_params=_cparams(("parallel",)),
        name="layer_norm",
    )(*args)


def _router_kernel(x_ref, r_ref, o_ref, *, n_experts):
    x_hi, x_lo = _split2(x_ref[...])
    r_hi, r_lo = _split2(r_ref[...])
    logits = _dot(x_hi, r_hi) + _dot(x_lo, r_hi) + _dot(x_hi, r_lo)
    lane = lax.broadcasted_iota(jnp.int32, logits.shape, 1).astype(F32)
    neg = -jnp.inf
    lg = jnp.where(lane < n_experts, logits, neg)
    m1 = jnp.max(lg, axis=1, keepdims=True)
    i1 = jnp.min(jnp.where(lg == m1, lane, float(LANE)), axis=1, keepdims=True)
    lg2 = jnp.where(lane == i1, neg, lg)
    m2 = jnp.max(lg2, axis=1, keepdims=True)
    i2 = jnp.min(jnp.where(lg2 == m2, lane, float(LANE)), axis=1, keepdims=True)
    e2 = jnp.exp(m2 - m1)
    w1 = 1.0 / (1.0 + e2)
    o_ref[...] = (jnp.where(lane == 0.0, i1, 0.0) + jnp.where(lane == 1.0, i2, 0.0)
                  + jnp.where(lane == 2.0, w1, 0.0) + jnp.where(lane == 3.0, e2 * w1, 0.0))


def _router_top2(x, router, tm=512):
    m, d = x.shape
    ne = router.shape[1]
    tm = _pick(m, tm)
    rpad = jnp.zeros((d, LANE), F32).at[:, :ne].set(router.astype(F32))
    return pl.pallas_call(
        functools.partial(_router_kernel, n_experts=ne),
        grid=(m // tm,),
        in_specs=[pl.BlockSpec((tm, d), lambda i: (i, 0)), pl.BlockSpec((d, LANE), lambda i: (0, 0))],
        out_specs=pl.BlockSpec((tm, LANE), lambda i: (i, 0)),
        out_shape=jax.ShapeDtypeStruct((m, LANE), F32),
        compiler_params=_cparams(("parallel",)),
        name="router",
    )(x, rpad)


MOE_TILE = 512
MOE_TOKENS = 256
TOP_K = 2


def _route_plan(top2, n_experts, tile):
    m = top2.shape[0]
    ef = top2[:, 0:TOP_K].astype(jnp.int32).reshape(-1)
    oh = (ef[:, None] == jnp.arange(n_experts, dtype=jnp.int32)[None, :]).astype(jnp.int32)
    rank = jnp.cumsum(oh, axis=0) - oh
    counts = jnp.sum(oh, axis=0)
    padded = ((counts + tile - 1) // tile) * tile
    ends = jnp.cumsum(padded)
    starts = ends - padded
    dest = jnp.sum(oh * (starts[None, :] + rank), axis=1).astype(jnp.int32)
    n_tiles = (TOP_K * m) // tile + n_experts
    tile_start = jnp.arange(n_tiles, dtype=jnp.int32) * tile
    tile_expert = jnp.sum((tile_start[:, None] >= ends[None, :]).astype(jnp.int32), axis=1)
    tile_expert = jnp.minimum(tile_expert, n_experts - 1).astype(jnp.int32)
    n_used = (ends[-1] // tile).astype(jnp.int32).reshape(1)
    return dest, tile_expert, n_used, n_tiles


def _dispatch_kernel(dest_ref, x_ref, _xg_in, xg_ref, sem, *, tq):
    base = pl.program_id(0) * tq

    def row_copy(r, slot):
        row = dest_ref[TOP_K * (base + r) + slot]
        return pltpu.make_async_copy(x_ref.at[pl.ds(r, 1)], xg_ref.at[pl.ds(row, 1)], sem)

    def issue(r, carry):
        for slot in range(TOP_K):
            row_copy(r, slot).start()
        return carry

    lax.fori_loop(0, tq, issue, 0, unroll=4)
    for _ in range(TOP_K):
        pltpu.make_async_copy(x_ref, xg_ref.at[pl.ds(0, tq)], sem).wait()


def _dispatch(xf, dest, n_rows, tq=MOE_TOKENS):
    m, d = xf.shape
    tq = _pick(m, tq)
    return pl.pallas_call(
        functools.partial(_dispatch_kernel, tq=tq),
        grid_spec=pltpu.PrefetchScalarGridSpec(
            num_scalar_prefetch=1,
            grid=(m // tq,),
            in_specs=[pl.BlockSpec((tq, d), lambda i, dest: (i, 0)), pl.BlockSpec(memory_space=pl.ANY)],
            out_specs=pl.BlockSpec(memory_space=pl.ANY),
            scratch_shapes=[pltpu.SemaphoreType.DMA(())],
        ),
        out_shape=jax.ShapeDtypeStruct((n_rows, d), F32),
        input_output_aliases={2: 0},
        compiler_params=_cparams(("arbitrary",)),
        name="moe_dispatch",
    )(dest, xf, jnp.zeros((n_rows, d), F32))


def _gswiglu_kernel(te_ref, nu_ref, a_ref, wg_ref, wu_ref, o_ref, abf_ref):
    i = pl.program_id(0)

    @pl.when(pl.program_id(1) == 0)
    def _():
        abf_ref[...] = a_ref[...].astype(BF16)

    @pl.when(i < nu_ref[0])
    def _():
        a = abf_ref[...]
        g = jnp.dot(a, wg_ref[...], preferred_element_type=F32)
        u = jnp.dot(a, wu_ref[...], preferred_element_type=F32)
        o_ref[...] = (g * _sigmoid(g) * u).astype(o_ref.dtype)

    @pl.when(i >= nu_ref[0])
    def _():
        o_ref[...] = jnp.zeros_like(o_ref)


def _gdown_kernel(te_ref, nu_ref, a_ref, w_ref, o_ref):
    i = pl.program_id(0)

    @pl.when(i < nu_ref[0])
    def _():
        o_ref[...] = jnp.dot(a_ref[...], w_ref[...], preferred_element_type=F32)

    @pl.when(i >= nu_ref[0])
    def _():
        o_ref[...] = jnp.zeros_like(o_ref)


def _grouped_experts(xg, wg, wu, wd, tile_expert, n_used, tile=MOE_TILE, tn=512, tn_down=1024):
    p, d = xg.shape
    f = wg.shape[2]
    tn, tn_down = _pick(f, tn), _pick(d, tn_down)
    nt = p // tile

    def wspec(rows, cols):
        return pl.BlockSpec((None, rows, cols),
                            lambda i, j, te, nu: (te[i], 0, jnp.where(i < nu[0], j, 0)))

    hdn = pl.pallas_call(
        _gswiglu_kernel,
        grid_spec=pltpu.PrefetchScalarGridSpec(
            num_scalar_prefetch=2,
            grid=(nt, f // tn),
            in_specs=[pl.BlockSpec((tile, d), lambda i, j, te, nu: (i, 0)), wspec(d, tn), wspec(d, tn)],
            out_specs=pl.BlockSpec((tile, tn), lambda i, j, te, nu: (i, j)),
            scratch_shapes=[pltpu.VMEM((tile, d), BF16)],
        ),
        out_shape=jax.ShapeDtypeStruct((p, f), BF16),
        compiler_params=_cparams(("arbitrary", "arbitrary")),
        name="moe_swiglu",
    )(tile_expert, n_used, xg, wg, wu)
    return pl.pallas_call(
        _gdown_kernel,
        grid_spec=pltpu.PrefetchScalarGridSpec(
            num_scalar_prefetch=2,
            grid=(nt, d // tn_down),
            in_specs=[pl.BlockSpec((tile, f), lambda i, j, te, nu: (i, 0)), wspec(f, tn_down)],
            out_specs=pl.BlockSpec((tile, tn_down), lambda i, j, te, nu: (i, j)),
        ),
        out_shape=jax.ShapeDtypeStruct((p, d), F32),
        compiler_params=_cparams(("arbitrary", "arbitrary")),
        name="moe_down",
    )(tile_expert, n_used, hdn, wd)


def _combine_ln_kernel(dest_ref, x_ref, top2_ref, g_ref, b_ref, yg_ref, of_ref, ob_ref, buf_ref, sem,
                       *, alpha, tq, nsteps):
    i = pl.program_id(0)
    slot = i % 2

    def fetch(step, slot_):
        base = step * tq

        def issue(r, carry):
            for k in range(TOP_K):
                row = dest_ref[TOP_K * (base + r) + k]
                pltpu.make_async_copy(yg_ref.at[pl.ds(row, 1)], buf_ref.at[slot_, k, pl.ds(r, 1)],
                                      sem.at[slot_]).start()
            return carry

        lax.fori_loop(0, tq, issue, 0, unroll=4)

    @pl.when(i == 0)
    def _():
        fetch(i, slot)

    @pl.when(i + 1 < nsteps)
    def _():
        fetch(i + 1, 1 - slot)

    for k in range(TOP_K):
        pltpu.make_async_copy(yg_ref.at[pl.ds(0, tq)], buf_ref.at[slot, k], sem.at[slot]).wait()
    top2 = top2_ref[...]
    ffn = top2[:, TOP_K:TOP_K + 1] * buf_ref[slot, 0]
    for k in range(1, TOP_K):
        ffn = ffn + top2[:, TOP_K + k:TOP_K + k + 1] * buf_ref[slot, k]
    x = alpha * x_ref[...] + ffn
    mu = jnp.mean(x, axis=-1, keepdims=True)
    xc = x - mu
    var = jnp.mean(xc * xc, axis=-1, keepdims=True)
    y = xc * lax.rsqrt(var + LN_EPS) * g_ref[...] + b_ref[...]
    of_ref[...] = y
    ob_ref[...] = y.astype(BF16)


def _combine_layer_norm(x, yg, top2, dest, g, b, alpha, tq=MOE_TOKENS):
    m, d = x.shape
    tq = _pick(m, tq)
    nsteps = m // tq
    row = pl.BlockSpec((tq, d), lambda i, dest: (i, 0))
    vec = pl.BlockSpec((1, d), lambda i, dest: (0, 0))
    return pl.pallas_call(
        functools.partial(_combine_ln_kernel, alpha=alpha, tq=tq, nsteps=nsteps),
        grid_spec=pltpu.PrefetchScalarGridSpec(
            num_scalar_prefetch=1,
            grid=(nsteps,),
            in_specs=[row, pl.BlockSpec((tq, LANE), lambda i, dest: (i, 0)), vec, vec,
                      pl.BlockSpec(memory_space=pl.ANY)],
            out_specs=[row, row],
            scratch_shapes=[pltpu.VMEM((2, TOP_K, tq, d), F32), pltpu.SemaphoreType.DMA((2,))],
        ),
        out_shape=[jax.ShapeDtypeStruct((m, d), F32), jax.ShapeDtypeStruct((m, d), BF16)],
        compiler_params=_cparams(("arbitrary",)),
        name="moe_combine_ln",
    )(dest, x, top2, g.reshape(1, d), b.reshape(1, d), yg)


def _dot_nt(a, b):
    return lax.dot_general(a, b, (((1,), (1,)), ((), ())), preferred_element_type=F32)


def _dot_tn(a, b):
    return lax.dot_general(a, b, (((0,), (0,)), ((), ())), preferred_element_type=F32)


def _dot_split_lhs(stat2, x):
    hi, lo = _split2(x)
    return _dot(stat2, jnp.concatenate([hi, lo], axis=0))


def _dot_split_rhs(x, stat):
    hi, lo = _split2(x)
    return _dot(hi, stat) + _dot(lo, stat)


def _log_sigmoid(z):
    return jnp.minimum(z, 0.0) - jnp.log(1.0 + jnp.exp(-jnp.abs(z)))


def _softplus(z):
    return jnp.maximum(z, 0.0) + jnp.log(1.0 + jnp.exp(-jnp.abs(z)))


def _n_levels(c):
    return int(round(math.log2(c)))


def _group_rows(gi, ngroups, unroll, rev):
    gg = (ngroups - 1 - gi) if rev else gi
    order = range(unroll - 1, -1, -1) if rev else range(unroll)
    return [pl.ds(pl.multiple_of((gg * unroll + u) * CHUNK, CHUNK), CHUNK) for u in order]


@functools.lru_cache(maxsize=None)
def _hg_static(c, rev):
    n = _n_levels(c)
    t = np.arange(c)[:, None]
    r = np.arange(c)[None, :]
    mats = [r <= t]
    d_mats, e_mats, masks = [], [], [t == r]
    for lvl in range(1, n + 1):
        blk, half = 1 << lvl, 1 << (lvl - 1)
        b0 = (t // blk) * blk
        mid = b0 + half - 1
        second = (t - b0) >= half
        d_mats.append(second & (r > mid) & (r <= t))
        e_mats.append((~second) & (r > t) & (r <= mid))
        masks.append((b0 == (r // blk) * blk) & second & ((r % blk) < half))
    mats = mats + d_mats + e_mats + [r > t]
    if rev:
        mats = [m[::-1, ::-1] for m in mats]
        masks = [m[::-1, ::-1] for m in masks]
    stat = np.concatenate(mats, axis=0).astype(np.float32)
    return stat, np.stack(masks).astype(np.float32)


def _hgrn2_kernel(*refs, rev, nchunk, unroll, epilogue, scale, side):
    if side:
        side_in_ref, *rest = refs
        n_in = 10 if epilogue else 7
        rest[n_in][...] = side_in_ref[...].astype(BF16)
        refs = rest[:n_in] + rest[n_in + 1:]
    if epilogue:
        (q_ref, z_ref, v_ref, llb_ref, l1m_ref, stat_ref, mask_ref,
         of_ref, og_ref, ng_ref, o_ref, st_ref) = refs
    else:
        q_ref, z_ref, v_ref, llb_ref, l1m_ref, stat_ref, mask_ref, o_ref, st_ref = refs
    c = CHUNK
    n = _n_levels(c)
    ngroups = nchunk // unroll

    @pl.when(pl.program_id(2) == 0)
    def _():
        st_ref[...] = jnp.zeros_like(st_ref)

    log_lb = llb_ref[...]
    log_1m_lb = l1m_ref[...]

    def group_step(gi, carry):
        rows_list = _group_rows(gi, ngroups, unroll, rev)
        nu = len(rows_list)
        q = [q_ref[rows, :] * scale for rows in rows_list]
        vb = [v_ref[rows, :].astype(BF16) for rows in rows_list]
        b = [log_1m_lb + _log_sigmoid(z_ref[rows, :]) for rows in rows_list]
        g = [jnp.maximum(log_lb, bi) + jnp.log(1.0 + jnp.exp(-jnp.abs(log_lb - bi))) for bi in b]
        k = [1.0 - jnp.exp(gi_) for gi_ in g]
        xall = _dot_split_lhs(stat_ref[...], jnp.concatenate(g, axis=1))

        def expo(blk, i):
            return xall[blk * c:(blk + 1) * c, i * LANE:(i + 1) * LANE]

        att = [mask_ref[0] * _dot_nt(q[i].astype(BF16), k[i].astype(BF16)) for i in range(nu)]
        for lvl in range(1, n + 1):
            ql = [(q[i] * jnp.exp(expo(lvl, i))).astype(BF16) for i in range(nu)]
            kl = [(k[i] * jnp.exp(expo(n + lvl, i))).astype(BF16) for i in range(nu)]
            att = [att[i] + mask_ref[lvl] * _dot_nt(ql[i], kl[i]) for i in range(nu)]
        o_intra = [_dot(att[i].astype(BF16), vb[i]) for i in range(nu)]
        qd = [(q[i] * jnp.exp(expo(0, i))).astype(BF16) for i in range(nu)]
        kd = [(k[i] * jnp.exp(expo(2 * n + 1, i))).astype(BF16) for i in range(nu)]
        dst = [_dot_tn(vb[i], kd[i]) for i in range(nu)]
        edge_row = 0 if rev else c - 1
        de = [jnp.exp(expo(0, i)[edge_row:edge_row + 1]) for i in range(nu)]
        states = [st_ref[...]]
        for i in range(nu):
            states.append(states[i] * de[i] + dst[i])
        st_ref[...] = states[nu]
        outs = [o_intra[i] + _dot_nt(qd[i], states[i].astype(BF16)) for i in range(nu)]
        for rows, o in zip(rows_list, outs):
            if epilogue:
                tot = o + of_ref[rows, :]
                y = tot * lax.rsqrt(jnp.mean(tot * tot, axis=-1, keepdims=True) + RMS_EPS) * ng_ref[...]
                gate = og_ref[rows, :]
                o_ref[rows, :] = (y * gate * _sigmoid(gate)).astype(o_ref.dtype)
            else:
                o_ref[rows, :] = o
        return carry

    lax.fori_loop(0, ngroups, group_step, 0)


def _hgrn2(p3, o_fwd, llb, l1m, norm_g, *, rev, heads, col, d_model, rows=2048, unroll=8, side=None):
    bsz, s, _ = p3.shape
    rows = _pick(s, rows)
    ns = s // rows
    nchunk = rows // CHUNK
    unroll = math.gcd(unroll, nchunk)
    stat, masks = _hg_static(CHUNK, rev)
    stat = jnp.asarray(np.concatenate([stat, stat], axis=1), BF16)
    masks = jnp.asarray(masks, F32)
    zcol = col["hg_f_bwd"] if rev else col["hg_f_fwd"]
    seq = (lambda i: ns - 1 - i) if rev else (lambda i: i)

    def pspec(c0):
        return pl.BlockSpec((None, rows, LANE), lambda b, h, i: (b, seq(i), c0 + h))

    vec = pl.BlockSpec((None, 1, LANE), lambda b, h, i: (h, 0, 0))
    in_specs = [pspec(col["hg_q"]), pspec(zcol), pspec(col["hg_i"]), vec, vec,
                pl.BlockSpec(stat.shape, lambda b, h, i: (0, 0)),
                pl.BlockSpec(masks.shape, lambda b, h, i: (0, 0, 0))]
    args = [p3, p3, p3, llb, l1m, stat, masks]
    if rev:
        in_specs += [pl.BlockSpec((None, rows, LANE), lambda b, h, i: (b, seq(i), h)),
                     pspec(col["hg_g"]),
                     pl.BlockSpec((1, LANE), lambda b, h, i: (0, 0))]
        args += [o_fwd, p3, norm_g]
    out_shape = jax.ShapeDtypeStruct((bsz, s, heads * LANE), BF16 if rev else F32)
    out_specs = pl.BlockSpec((None, rows, LANE), lambda b, h, i: (b, seq(i), h))
    if side is not None:
        nsteps = bsz * heads * ns
        slab = (side.size // side.shape[-1]) // nsteps
        side3 = side.reshape(nsteps, slab, side.shape[-1])
        side_spec = pl.BlockSpec((None,) + side3.shape[1:], lambda b, h, i: ((b * heads + h) * ns + i, 0, 0))
        in_specs, args = [side_spec] + in_specs, [side3] + args
        out_specs, out_shape = [side_spec, out_specs], [jax.ShapeDtypeStruct(side3.shape, BF16), out_shape]
    res = pl.pallas_call(
        functools.partial(_hgrn2_kernel, rev=rev, nchunk=nchunk, unroll=unroll, epilogue=rev,
                          scale=HEAD_DIM ** -0.5, side=side is not None),
        grid=(bsz, heads, ns),
        in_specs=in_specs,
        out_specs=out_specs,
        out_shape=out_shape,
        scratch_shapes=[pltpu.VMEM((HEAD_DIM, HEAD_DIM), F32)],
        compiler_params=_cparams(("parallel", "parallel", "arbitrary")),
        name="hgrn2_bwd" if rev else "hgrn2_fwd",
    )(*args)
    if side is None:
        return res, None
    return res[1], res[0].reshape(side.shape)


def _gd_prep_kernel(x_ref, w_ref, o_ref, xp_ref, *, s, tile, n_qk, n_q, scale_q):
    j = pl.program_id(1)
    halo = 8
    xp_ref[0:halo, :] = jnp.zeros((halo, LANE), F32)
    xp_ref[halo + s:2 * halo + s, :] = jnp.zeros((halo, LANE), F32)
    xp_ref[halo:halo + s, :] = x_ref[...]
    w = w_ref[...]
    inv_scale = jnp.where(j < n_q, scale_q, 1.0)

    def tile_step(ti, carry):
        r0 = pl.multiple_of(ti * tile, tile)
        win = xp_ref[pl.ds(r0, tile + 2 * halo), :]
        acc = jnp.zeros((tile, LANE), F32)
        for tap in range(CONV_K):
            off = halo + tap - CONV_K // 2
            acc = acc + w[tap:tap + 1, :] * win[off:off + tile, :]
        y = acc * _sigmoid(acc)
        inv = lax.rsqrt(jnp.sum(y * y, axis=-1, keepdims=True) + RMS_EPS) * inv_scale
        o_ref[pl.ds(r0, tile), :] = y * jnp.where(j < n_qk, inv, 1.0)
        return carry

    lax.fori_loop(0, s // tile, tile_step, 0)


def _gd_prep(p3, conv_w, *, col0, heads, tile=1024):
    bsz, s, _ = p3.shape
    tile = _pick(s, tile)
    nblk = 3 * heads
    return pl.pallas_call(
        functools.partial(_gd_prep_kernel, s=s, tile=tile, n_qk=2 * heads, n_q=heads, scale_q=HEAD_DIM ** -0.5),
        grid=(bsz, nblk),
        in_specs=[pl.BlockSpec((None, s, LANE), lambda b, j: (b, 0, col0 + j)),
                  pl.BlockSpec((CONV_K, LANE), lambda b, j: (0, j))],
        out_specs=pl.BlockSpec((None, s, LANE), lambda b, j: (b, 0, j)),
        out_shape=jax.ShapeDtypeStruct((bsz, s, nblk * LANE), F32),
        scratch_shapes=[pltpu.VMEM((s + 16, LANE), F32)],
        compiler_params=_cparams(("parallel", "parallel")),
        name="gd_prep",
    )(p3, conv_w)


@functools.lru_cache(maxsize=None)
def _gd_static(c, rev):
    t = np.arange(c)[:, None]
    s = np.arange(c)[None, :]
    incl = (s >= t) if rev else (s <= t)
    strict = (s > t) if rev else (s < t)
    mats = [incl, strict, t == s, np.ones((c, c), bool), strict & ((t // 8) == (s // 8))]
    blk = 16
    while blk <= c:
        half = blk // 2
        same = (t // blk) == (s // blk)
        t2, s2 = (t % blk) >= half, (s % blk) >= half
        mats.append(same & ((~t2) & s2 if rev else t2 & (~s2)))
        blk *= 2
    return np.stack(mats).astype(np.float32)


def _gdn_kernel(*refs, rev, nchunk, unroll, epilogue, heads, hb):
    if epilogue:
        (q_ref, k_ref, v_ref, pg_ref, na_ref, dtb_ref, stat_ref, stat2_ref,
         of_ref, og_ref, ng_ref, o_ref, s_ref, gb_ref, bb_ref) = refs
    else:
        q_ref, k_ref, v_ref, pg_ref, na_ref, dtb_ref, stat_ref, stat2_ref, o_ref, s_ref, gb_ref, bb_ref = refs
    c = CHUNK
    ngroups = nchunk // unroll
    hg = pl.program_id(1)

    @pl.when(pl.program_id(2) == 0)
    def _():
        s_ref[...] = jnp.zeros_like(s_ref)

    incl = stat_ref[0]
    strict = stat_ref[1]
    eye = stat_ref[2]
    incl2 = stat2_ref[0]
    ones2 = stat2_ref[1]
    n_lvl = stat_ref.shape[0] - 5

    pg = pg_ref[...]
    glog = na_ref[...] * _softplus(pg + dtb_ref[...])
    beta = _sigmoid(pg)
    cum_all = jnp.concatenate(
        [_dot_split_lhs(incl2, glog[ci * c:(ci + 1) * c]) for ci in range(nchunk)], axis=0)
    rsel = lax.broadcasted_iota(jnp.int32, (LANE, LANE), 0)
    d = 1 if rev else 0
    for hh in range(hb):
        h = hg * hb + hh
        gb_ref[hh] = _dot_split_rhs(cum_all, jnp.where(rsel == d * heads + h, 1.0, 0.0).astype(BF16))
        bb_ref[hh] = _dot(beta.astype(BF16), jnp.where(rsel == (2 + d) * heads + h, 1.0, 0.0).astype(BF16))

    d8 = stat_ref[4]
    edge_row = 0 if rev else c - 1

    def group_step(gi, carry):
        rows_list = _group_rows(gi, ngroups, unroll, rev)
        items = [(rows, hh) for rows in rows_list for hh in range(hb)]
        ni = len(items)
        rng = range(ni)
        lanes = [slice(hh * LANE, (hh + 1) * LANE) for _, hh in items]
        q = [q_ref[rows, lanes[i]] for i, (rows, _) in enumerate(items)]
        k = [k_ref[rows, lanes[i]] for i, (rows, _) in enumerate(items)]
        v = [v_ref[rows, lanes[i]] for i, (rows, _) in enumerate(items)]
        cum = [gb_ref[hh, rows, :] for rows, hh in items]
        bb = [bb_ref[hh, rows, :] for rows, hh in items]
        cum_row = [_dot_split_lhs(ones2, cum[i][:, 0:c] * eye) for i in rng]
        decay = [jnp.exp(jnp.where(incl > 0, cum[i][:, 0:c] - cum_row[i], -jnp.inf)) for i in rng]
        ecum = [jnp.exp(cum[i]) for i in rng]
        kb = [k[i] * bb[i] for i in rng]
        kbf = [k[i].astype(BF16) for i in rng]
        a = [strict * _dot_nt(kb[i].astype(BF16), kbf[i]) * decay[i] for i in rng]
        qk = [(_dot_nt(q[i].astype(BF16), kbf[i]) * decay[i]).astype(BF16) for i in rng]
        a0 = [a[i] * d8 for i in rng]
        a0b = [a0[i].astype(BF16) for i in rng]
        a2b = [_dot(a0b[i], a0b[i]).astype(BF16) for i in rng]
        xm = [eye - a0[i] for i in rng]
        ym = [xm[i] + _dot(xm[i].astype(BF16), a2b[i]) for i in rng]
        a4b = [_dot(a2b[i], a2b[i]).astype(BF16) for i in rng]
        tm = [ym[i] + _dot(ym[i].astype(BF16), a4b[i]) for i in rng]
        for lvl in range(n_lvl):
            off = stat_ref[5 + lvl]
            tmb = [tm[i].astype(BF16) for i in rng]
            mid = [_dot(tmb[i], (a[i] * off).astype(BF16)).astype(BF16) for i in rng]
            tm = [tm[i] - _dot(mid[i], tmb[i]) for i in rng]
        tmb = [tm[i].astype(BF16) for i in rng]
        u = [_dot(tmb[i], (v[i] * bb[i]).astype(BF16)) for i in rng]
        w = [_dot(tmb[i], (kb[i] * ecum[i]).astype(BF16)).astype(BF16) for i in rng]
        edge = [cum[i][edge_row:edge_row + 1] for i in rng]
        qd = [(q[i] * ecum[i]).astype(BF16) for i in rng]
        kd = [(k[i] * jnp.exp(edge[i] - cum[i])).astype(BF16) for i in rng]
        de = [jnp.exp(edge[i]) for i in rng]
        st = [s_ref[hh] for hh in range(hb)]
        heads_rng = range(hb)
        for ui, rows in enumerate(rows_list):
            idx = [ui * hb + hh for hh in heads_rng]
            stb = [st[hh].astype(BF16) for hh in heads_rng]
            ws = [_dot(w[idx[hh]], stb[hh]) for hh in heads_rng]
            oq = [_dot(qd[idx[hh]], stb[hh]) for hh in heads_rng]
            vnb = [(u[idx[hh]] - ws[hh]).astype(BF16) for hh in heads_rng]
            dst = [_dot_tn(kd[idx[hh]], vnb[hh]) for hh in heads_rng]
            o = [oq[hh] + _dot(qk[idx[hh]], vnb[hh]) for hh in heads_rng]
            st = [st[hh] * de[idx[hh]] + dst[hh] for hh in heads_rng]
            for hh in heads_rng:
                ln = slice(hh * LANE, (hh + 1) * LANE)
                if epilogue:
                    tot = o[hh] + of_ref[rows, ln]
                    y = tot * lax.rsqrt(jnp.mean(tot * tot, axis=-1, keepdims=True) + RMS_EPS) * ng_ref[...]
                    gate = og_ref[rows, ln]
                    o_ref[rows, ln] = (y * gate * _sigmoid(gate)).astype(o_ref.dtype)
                else:
                    o_ref[rows, ln] = o[hh]
        for hh in heads_rng:
            s_ref[hh] = st[hh]
        return carry

    lax.fori_loop(0, ngroups, group_step, 0)


def _gdn(qkv, pg, p3, o_fwd, neg_a, dtb, norm_g, *, rev, heads, gcol, rows=512, unroll=4, hb=8):
    bsz, s, _ = qkv.shape
    rows = _pick(s, rows)
    ns = s // rows
    nchunk = rows // CHUNK
    unroll = math.gcd(unroll, nchunk)
    hb = math.gcd(hb, heads)
    stat_np = _gd_static(CHUNK, rev)
    stat = jnp.asarray(stat_np, F32)
    stat2 = jnp.asarray(np.stack([np.concatenate([stat_np[i], stat_np[i]], axis=1) for i in (0, 3)]), BF16)
    seq = (lambda i: ns - 1 - i) if rev else (lambda i: i)
    wide = hb * LANE

    def qspec(c0):
        return pl.BlockSpec((None, rows, wide), lambda b, h, i: (b, seq(i), c0 // hb + h))

    vec = pl.BlockSpec((1, LANE), lambda b, h, i: (0, 0))
    in_specs = [qspec(0), qspec(heads), qspec(2 * heads),
                pl.BlockSpec((None, rows, LANE), lambda b, h, i: (b, seq(i), 0)),
                vec, vec, pl.BlockSpec(stat.shape, lambda b, h, i: (0, 0, 0)),
                pl.BlockSpec(stat2.shape, lambda b, h, i: (0, 0, 0))]
    args = [qkv, qkv, qkv, pg, neg_a, dtb, stat, stat2]
    if rev:
        in_specs += [qspec(0), qspec(gcol), vec]
        args += [o_fwd, p3, norm_g]
    out_shape = jax.ShapeDtypeStruct((bsz, s, heads * LANE), BF16 if rev else F32)
    out_spec = qspec(0)
    return pl.pallas_call(
        functools.partial(_gdn_kernel, rev=rev, nchunk=nchunk, unroll=unroll, epilogue=rev, heads=heads, hb=hb),
        grid=(bsz, heads // hb, ns),
        in_specs=in_specs,
        out_specs=out_spec,
        out_shape=out_shape,
        scratch_shapes=[pltpu.VMEM((hb, HEAD_DIM, HEAD_DIM), F32),
                        pltpu.VMEM((hb, rows, LANE), F32), pltpu.VMEM((hb, rows, LANE), F32)],
        compiler_params=_cparams(("parallel", "parallel", "arbitrary")),
        name="gdn_bwd" if rev else "gdn_fwd",
    )(*args)


def _pad_cols(w, n):
    return jnp.pad(w, ((0, 0),) * (w.ndim - 1) + ((0, n - w.shape[-1]),))


def kernel(x, ln_in_g, ln_in_b, w_in, hg_lb_param, gd_conv_w, gd_a_log, gd_dt_bias, hg_norm_g, gd_norm_g, w_out,
           ln1_g, ln1_b, ffn_w_gate, ffn_w_up, ffn_w_down, moe_router, moe_w_gate, moe_w_up, moe_w_down,
           ln2_g, ln2_b):
    bsz, s, d = x.shape
    depth = w_in.shape[0]
    m = bsz * s
    half = d // 2
    heads = half // HEAD_DIM
    alpha = (2 * depth) ** 0.25
    n_main = 8 * half
    n_gate = N_GATE_COLS * heads
    names = ("hg_q", "hg_f_fwd", "hg_f_bwd", "hg_i", "hg_g", "gd_q", "gd_k", "gd_v", "gd_g")
    col = {nm: i * heads for i, nm in enumerate(names)}

    lb = jnp.cumsum(jax.nn.softmax(hg_lb_param.astype(F32), axis=0), axis=0)
    lb = lb - lb[0:1]
    log_lb = jnp.log(lb).reshape(depth, 2, heads, 1, LANE)
    log_1m_lb = jnp.log1p(-lb).reshape(depth, 2, heads, 1, LANE)

    to_cast = [(name, jj) for jj in range(moe_w_gate.shape[0]) for name in ("gate", "up", "down")]
    moe_f32 = {"gate": moe_w_gate, "up": moe_w_up, "down": moe_w_down}
    moe_bf16 = {}

    def scan_with_cast(layer, *args, **kwargs):
        nsteps = bsz * heads * (s // _pick(s, 2048))
        for idx, (name, jj) in enumerate(to_cast):
            w = moe_f32[name][jj]
            if 2 * jj + 1 >= layer and (w.size // w.shape[-1]) % (16 * nsteps) == 0:
                to_cast.pop(idx)
                out, moe_bf16[name, jj] = _hgrn2(*args, side=w, **kwargs)
                return out
        return _hgrn2(*args, **kwargs)[0]

    xf, xb = _layer_norm(x.reshape(m, d), None, ln_in_g, ln_in_b)
    for l in range(depth):
        w_l = w_in[l]
        w_main = jnp.concatenate([w_l[:, :n_main], w_l[:, n_main + n_gate:]], axis=1).astype(BF16)
        w_gate = _pad_cols(w_l[:, n_main:n_main + n_gate], LANE).astype(BF16)
        p3 = _matmul(xb, w_main, F32, tn=1024, name="in_proj").reshape(bsz, s, -1)
        pg = _matmul(xb, w_gate, F32, tn=LANE, name="gate_proj").reshape(bsz, s, LANE)

        o_f = scan_with_cast(l, p3, None, log_lb[l, 0], log_1m_lb[l, 0], None,
                             rev=False, heads=heads, col=col, d_model=d)
        hg_out = scan_with_cast(l, p3, o_f, log_lb[l, 1], log_1m_lb[l, 1], hg_norm_g[l].reshape(1, LANE).astype(F32),
                                rev=True, heads=heads, col=col, d_model=d)

        qkv = _gd_prep(p3, gd_conv_w[l].astype(F32), col0=col["gd_q"], heads=heads)
        neg_a = jnp.zeros((1, LANE), F32).at[0, :2 * heads].set(-jnp.exp(gd_a_log[l].astype(F32)).reshape(-1))
        dtb = jnp.zeros((1, LANE), F32).at[0, :2 * heads].set(gd_dt_bias[l].astype(F32).reshape(-1))
        gn = gd_norm_g[l].reshape(1, LANE).astype(F32)
        g_f = _gdn(qkv, pg, None, None, neg_a, dtb, None, rev=False, heads=heads, gcol=col["gd_g"])
        gd_out = _gdn(qkv, pg, p3, g_f, neg_a, dtb, gn, rev=True, heads=heads, gcol=col["gd_g"])

        mix = _matmul_cat(hg_out.reshape(m, half), gd_out.reshape(m, half), w_out[l].astype(BF16), F32,
                          name="out_proj")
        xf, xb = _layer_norm(xf, mix, ln1_g[l], ln1_b[l], alpha=alpha)

        j = l // 2
        if l % 2 == 0:
            f = ffn_w_gate.shape[-1]
            fp = -(-f // 1024) * 1024
            hdn = _swiglu(xb, _pad_cols(ffn_w_gate[j], fp).astype(BF16), _pad_cols(ffn_w_up[j], fp).astype(BF16))
            wd = jnp.pad(ffn_w_down[j], ((0, fp - f), (0, 0))).astype(BF16)
            ffn = _matmul(hdn, wd, F32, tn=1024, tk=2816, name="down_proj")
            xf, xb = _layer_norm(xf, ffn, ln2_g[l], ln2_b[l], alpha=alpha)
        else:
            n_experts = moe_router.shape[-1]
            top2 = _router_top2(xf, moe_router[j])
            dest, tile_expert, n_used, n_tiles = _route_plan(top2, n_experts, MOE_TILE)
            xg = _dispatch(xf, dest, n_tiles * MOE_TILE)
            wts = [moe_bf16[name, j] if (name, j) in moe_bf16 else moe_f32[name][j].astype(BF16)
                   for name in ("gate", "up", "down")]
            yg = _grouped_experts(xg, *wts, tile_expert, n_used)
            xf, xb = _combine_layer_norm(xf, yg, top2, dest, ln2_g[l], ln2_b[l], alpha)
    return xf.reshape(bsz, s, d)
```

```python
import functools
import math

import numpy as np
import jax
import jax.numpy as jnp
from jax import lax
from jax.experimental import pallas as pl
from jax.experimental.pallas import tpu as pltpu

F32 = jnp.float32
BF16 = jnp.bfloat16

LANE = 128
HEAD_DIM = 128
CHUNK = 64
CONV_K = 5
N_GATE_COLS = 4
LN_EPS = 1e-5
RMS_EPS = 1e-6
VMEM_LIMIT_BYTES = 56 * 1024 * 1024


def _cparams(sem):
    return pltpu.CompilerParams(dimension_semantics=sem, vmem_limit_bytes=VMEM_LIMIT_BYTES)


def _pick(n, pref):
    t = min(n, pref)
    while n % t:
        t -= LANE
    return t


def _mm_kernel(a_ref, b_ref, o_ref):
    o_ref[...] = jnp.dot(a_ref[...], b_ref[...], preferred_element_type=F32).astype(o_ref.dtype)


def _mm_acc_kernel(a_ref, b_ref, o_ref, acc_ref, *, nk):
    k = pl.program_id(2)
    prod = jnp.dot(a_ref[...], b_ref[...], preferred_element_type=F32)

    @pl.when(k == 0)
    def _():
        acc_ref[...] = prod

    @pl.when(k > 0)
    def _():
        acc_ref[...] += prod

    @pl.when(k == nk - 1)
    def _():
        o_ref[...] = acc_ref[...].astype(o_ref.dtype)


def _matmul(a, b, out_dtype, tm=1024, tn=512, tk=None, name="matmul"):
    m, kdim = a.shape
    n = b.shape[1]
    tm, tn = _pick(m, tm), _pick(n, tn)
    tk = kdim if tk is None else _pick(kdim, tk)
    nk = kdim // tk
    if nk == 1:
        return pl.pallas_call(
            _mm_kernel,
            grid=(m // tm, n // tn),
            in_specs=[pl.BlockSpec((tm, kdim), lambda i, j: (i, 0)),
                      pl.BlockSpec((kdim, tn), lambda i, j: (0, j))],
            out_specs=pl.BlockSpec((tm, tn), lambda i, j: (i, j)),
            out_shape=jax.ShapeDtypeStruct((m, n), out_dtype),
            compiler_params=_cparams(("parallel", "arbitrary")),
            name=name,
        )(a, b)
    return pl.pallas_call(
        functools.partial(_mm_acc_kernel, nk=nk),
        grid=(m // tm, n // tn, nk),
        in_specs=[pl.BlockSpec((tm, tk), lambda i, j, k: (i, k)),
                  pl.BlockSpec((tk, tn), lambda i, j, k: (k, j))],
        out_specs=pl.BlockSpec((tm, tn), lambda i, j, k: (i, j)),
        out_shape=jax.ShapeDtypeStruct((m, n), out_dtype),
        scratch_shapes=[pltpu.VMEM((tm, tn), F32)],
        compiler_params=_cparams(("parallel", "arbitrary", "arbitrary")),
        name=name,
    )(a, b)


def _mm2_kernel(a1_ref, a2_ref, b1_ref, b2_ref, o_ref):
    acc = jnp.dot(a1_ref[...], b1_ref[...], preferred_element_type=F32)
    acc = acc + jnp.dot(a2_ref[...], b2_ref[...], preferred_element_type=F32)
    o_ref[...] = acc.astype(o_ref.dtype)


def _matmul_cat(a1, a2, b, out_dtype, tm=1024, tn=512, name="matmul_cat"):
    m, kh = a1.shape
    n = b.shape[1]
    tm, tn = _pick(m, tm), _pick(n, tn)
    return pl.pallas_call(
        _mm2_kernel,
        grid=(m // tm, n // tn),
        in_specs=[pl.BlockSpec((tm, kh), lambda i, j: (i, 0)), pl.BlockSpec((tm, kh), lambda i, j: (i, 0)),
                  pl.BlockSpec((kh, tn), lambda i, j: (0, j)), pl.BlockSpec((kh, tn), lambda i, j: (1, j))],
        out_specs=pl.BlockSpec((tm, tn), lambda i, j: (i, j)),
        out_shape=jax.ShapeDtypeStruct((m, n), out_dtype),
        compiler_params=_cparams(("parallel", "arbitrary")),
        name=name,
    )(a1, a2, b, b)


def _sigmoid(x):
    return 1.0 / (1.0 + jnp.exp(-x))


def _dot(a, b):
    return jnp.dot(a, b, preferred_element_type=F32)


def _split2(x):
    hi = x.astype(BF16)
    return hi, (x - hi.astype(F32)).astype(BF16)


def _swiglu_kernel(a_ref, wg_ref, wu_ref, o_ref):
    a = a_ref[...]
    g = jnp.dot(a, wg_ref[...], preferred_element_type=F32)
    u = jnp.dot(a, wu_ref[...], preferred_element_type=F32)
    o_ref[...] = (g * _sigmoid(g) * u).astype(o_ref.dtype)


def _swiglu(a, wg, wu, tm=1024, tn=512, name="swiglu"):
    m, kdim = a.shape
    f = wg.shape[1]
    tm, tn = _pick(m, tm), _pick(f, tn)
    wspec = pl.BlockSpec((kdim, tn), lambda i, j: (0, j))
    return pl.pallas_call(
        _swiglu_kernel,
        grid=(m // tm, f // tn),
        in_specs=[pl.BlockSpec((tm, kdim), lambda i, j: (i, 0)), wspec, wspec],
        out_specs=pl.BlockSpec((tm, tn), lambda i, j: (i, j)),
        out_shape=jax.ShapeDtypeStruct((m, f), BF16),
        compiler_params=_cparams(("parallel", "arbitrary")),
        name=name,
    )(a, wg, wu)


def _ln_kernel(*refs, alpha, has_res):
    if has_res:
        x_ref, r_ref, g_ref, b_ref, of_ref, ob_ref = refs
        x = alpha * x_ref[...] + r_ref[...]
    else:
        x_ref, g_ref, b_ref, of_ref, ob_ref = refs
        x = x_ref[...]
    mu = jnp.mean(x, axis=-1, keepdims=True)
    xc = x - mu
    var = jnp.mean(xc * xc, axis=-1, keepdims=True)
    y = xc * lax.rsqrt(var + LN_EPS) * g_ref[...] + b_ref[...]
    of_ref[...] = y
    ob_ref[...] = y.astype(BF16)


def _layer_norm(x, res, g, b, alpha=1.0, tm=256):
    m, d = x.shape
    tm = _pick(m, tm)
    row = pl.BlockSpec((tm, d), lambda i: (i, 0))
    vec = pl.BlockSpec((1, d), lambda i: (0, 0))
    args = [x] + ([res] if res is not None else []) + [g.reshape(1, d), b.reshape(1, d)]
    return pl.pallas_call(
        functools.partial(_ln_kernel, alpha=alpha, has_res=res is not None),
        grid=(m // tm,),
        in_specs=[row] * (len(args) - 2) + [vec, vec],
        out_specs=[row, row],
        out_shape=[jax.ShapeDtypeStruct((m, d), F32), jax.ShapeDtypeStruct((m, d), BF16)],
        compiler_params=_cparams(("parallel",)),
        name="layer_norm",
    )(*args)


def _router_kernel(x_ref, r_ref, o_ref, *, n_experts):
    x_hi, x_lo = _split2(x_ref[...])
    r_hi, r_lo = _split2(r_ref[...])
    logits = _dot(x_hi, r_hi) + _dot(x_lo, r_hi) + _dot(x_hi, r_lo)
    lane = lax.broadcasted_iota(jnp.int32, logits.shape, 1).astype(F32)
    neg = -jnp.inf
    lg = jnp.where(lane < n_experts, logits, neg)
    m1 = jnp.max(lg, axis=1, keepdims=True)
    i1 = jnp.min(jnp.where(lg == m1, lane, float(LANE)), axis=1, keepdims=True)
    lg2 = jnp.where(lane == i1, neg, lg)
    m2 = jnp.max(lg2, axis=1, keepdims=True)
    i2 = jnp.min(jnp.where(lg2 == m2, lane, float(LANE)), axis=1, keepdims=True)
    e2 = jnp.exp(m2 - m1)
    w1 = 1.0 / (1.0 + e2)
    o_ref[...] = (jnp.where(lane == 0.0, i1, 0.0) + jnp.where(lane == 1.0, i2, 0.0)
                  + jnp.where(lane == 2.0, w1, 0.0) + jnp.where(lane == 3.0, e2 * w1, 0.0))


def _router_top2(x, router, tm=512):
    m, d = x.shape
    ne = router.shape[1]
    tm = _pick(m, tm)
    rpad = jnp.zeros((d, LANE), F32).at[:, :ne].set(router.astype(F32))
    return pl.pallas_call(
        functools.partial(_router_kernel, n_experts=ne),
        grid=(m // tm,),
        in_specs=[pl.BlockSpec((tm, d), lambda i: (i, 0)), pl.BlockSpec((d, LANE), lambda i: (0, 0))],
        out_specs=pl.BlockSpec((tm, LANE), lambda i: (i, 0)),
        out_shape=jax.ShapeDtypeStruct((m, LANE), F32),
        compiler_params=_cparams(("parallel",)),
        name="router",
    )(x, rpad)


MOE_TILE = 512
MOE_TOKENS = 256
TOP_K = 2


def _route_plan(top2, n_experts, tile):
    m = top2.shape[0]
    ef = top2[:, 0:TOP_K].astype(jnp.int32).reshape(-1)
    oh = (ef[:, None] == jnp.arange(n_experts, dtype=jnp.int32)[None, :]).astype(jnp.int32)
    rank = jnp.cumsum(oh, axis=0) - oh
    counts = jnp.sum(oh, axis=0)
    padded = ((counts + tile - 1) // tile) * tile
    ends = jnp.cumsum(padded)
    starts = ends - padded
    dest = jnp.sum(oh * (starts[None, :] + rank), axis=1).astype(jnp.int32)
    n_tiles = (TOP_K * m) // tile + n_experts
    tile_start = jnp.arange(n_tiles, dtype=jnp.int32) * tile
    tile_expert = jnp.sum((tile_start[:, None] >= ends[None, :]).astype(jnp.int32), axis=1)
    tile_expert = jnp.minimum(tile_expert, n_experts - 1).astype(jnp.int32)
    n_used = (ends[-1] // tile).astype(jnp.int32).reshape(1)
    return dest, tile_expert, n_used, n_tiles


def _dispatch_kernel(dest_ref, x_ref, _xg_in, xg_ref, sem, *, tq):
    base = pl.program_id(0) * tq

    def row_copy(r, slot):
        row = dest_ref[TOP_K * (base + r) + slot]
        return pltpu.make_async_copy(x_ref.at[pl.ds(r, 1)], xg_ref.at[pl.ds(row, 1)], sem)

    def issue(r, carry):
        for slot in range(TOP_K):
            row_copy(r, slot).start()
        return carry

    lax.fori_loop(0, tq, issue, 0, unroll=4)
    for _ in range(TOP_K):
        pltpu.make_async_copy(x_ref, xg_ref.at[pl.ds(0, tq)], sem).wait()


def _dispatch(xf, dest, n_rows, tq=MOE_TOKENS):
    m, d = xf.shape
    tq = _pick(m, tq)
    return pl.pallas_call(
        functools.partial(_dispatch_kernel, tq=tq),
        grid_spec=pltpu.PrefetchScalarGridSpec(
            num_scalar_prefetch=1,
            grid=(m // tq,),
            in_specs=[pl.BlockSpec((tq, d), lambda i, dest: (i, 0)), pl.BlockSpec(memory_space=pl.ANY)],
            out_specs=pl.BlockSpec(memory_space=pl.ANY),
            scratch_shapes=[pltpu.SemaphoreType.DMA(())],
        ),
        out_shape=jax.ShapeDtypeStruct((n_rows, d), F32),
        input_output_aliases={2: 0},
        compiler_params=_cparams(("arbitrary",)),
        name="moe_dispatch",
    )(dest, xf, jnp.zeros((n_rows, d), F32))


def _gswiglu_kernel(te_ref, nu_ref, a_ref, wg_ref, wu_ref, o_ref, abf_ref):
    i = pl.program_id(0)

    @pl.when(pl.program_id(1) == 0)
    def _():
        abf_ref[...] = a_ref[...].astype(BF16)

    @pl.when(i < nu_ref[0])
    def _():
        a = abf_ref[...]
        g = jnp.dot(a, wg_ref[...], preferred_element_type=F32)
        u = jnp.dot(a, wu_ref[...], preferred_element_type=F32)
        o_ref[...] = (g * _sigmoid(g) * u).astype(o_ref.dtype)

    @pl.when(i >= nu_ref[0])
    def _():
        o_ref[...] = jnp.zeros_like(o_ref)


def _gdown_kernel(te_ref, nu_ref, a_ref, w_ref, o_ref):
    i = pl.program_id(0)

    @pl.when(i < nu_ref[0])
    def _():
        o_ref[...] = jnp.dot(a_ref[...], w_ref[...], preferred_element_type=F32)

    @pl.when(i >= nu_ref[0])
    def _():
        o_ref[...] = jnp.zeros_like(o_ref)


def _grouped_experts(xg, wg, wu, wd, tile_expert, n_used, tile=MOE_TILE, tn=512, tn_down=1024):
    p, d = xg.shape
    f = wg.shape[2]
    tn, tn_down = _pick(f, tn), _pick(d, tn_down)
    nt = p // tile

    def wspec(rows, cols):
        return pl.BlockSpec((None, rows, cols),
                            lambda i, j, te, nu: (te[i], 0, jnp.where(i < nu[0], j, 0)))

    hdn = pl.pallas_call(
        _gswiglu_kernel,
        grid_spec=pltpu.PrefetchScalarGridSpec(
            num_scalar_prefetch=2,
            grid=(nt, f // tn),
            in_specs=[pl.BlockSpec((tile, d), lambda i, j, te, nu: (i, 0)), wspec(d, tn), wspec(d, tn)],
            out_specs=pl.BlockSpec((tile, tn), lambda i, j, te, nu: (i, j)),
            scratch_shapes=[pltpu.VMEM((tile, d), BF16)],
        ),
        out_shape=jax.ShapeDtypeStruct((p, f), BF16),
        compiler_params=_cparams(("arbitrary", "arbitrary")),
        name="moe_swiglu",
    )(tile_expert, n_used, xg, wg, wu)
    return pl.pallas_call(
        _gdown_kernel,
        grid_spec=pltpu.PrefetchScalarGridSpec(
            num_scalar_prefetch=2,
            grid=(nt, d // tn_down),
            in_specs=[pl.BlockSpec((tile, f), lambda i, j, te, nu: (i, 0)), wspec(f, tn_down)],
            out_specs=pl.BlockSpec((tile, tn_down), lambda i, j, te, nu: (i, j)),
        ),
        out_shape=jax.ShapeDtypeStruct((p, d), F32),
        compiler_params=_cparams(("arbitrary", "arbitrary")),
        name="moe_down",
    )(tile_expert, n_used, hdn, wd)


def _combine_ln_kernel(dest_ref, x_ref, top2_ref, g_ref, b_ref, yg_ref, of_ref, ob_ref, buf_ref, sem,
                       *, alpha, tq, nsteps):
    i = pl.program_id(0)
    slot = i % 2

    def fetch(step, slot_):
        base = step * tq

        def issue(r, carry):
            for k in range(TOP_K):
                row = dest_ref[TOP_K * (base + r) + k]
                pltpu.make_async_copy(yg_ref.at[pl.ds(row, 1)], buf_ref.at[slot_, k, pl.ds(r, 1)],
                                      sem.at[slot_]).start()
            return carry

        lax.fori_loop(0, tq, issue, 0, unroll=4)

    @pl.when(i == 0)
    def _():
        fetch(i, slot)

    @pl.when(i + 1 < nsteps)
    def _():
        fetch(i + 1, 1 - slot)

    for k in range(TOP_K):
        pltpu.make_async_copy(yg_ref.at[pl.ds(0, tq)], buf_ref.at[slot, k], sem.at[slot]).wait()
    top2 = top2_ref[...]
    ffn = top2[:, TOP_K:TOP_K + 1] * buf_ref[slot, 0]
    for k in range(1, TOP_K):
        ffn = ffn + top2[:, TOP_K + k:TOP_K + k + 1] * buf_ref[slot, k]
    x = alpha * x_ref[...] + ffn
    mu = jnp.mean(x, axis=-1, keepdims=True)
    xc = x - mu
    var = jnp.mean(xc * xc, axis=-1, keepdims=True)
    y = xc * lax.rsqrt(var + LN_EPS) * g_ref[...] + b_ref[...]
    of_ref[...] = y
    ob_ref[...] = y.astype(BF16)


def _combine_layer_norm(x, yg, top2, dest, g, b, alpha, tq=MOE_TOKENS):
    m, d = x.shape
    tq = _pick(m, tq)
    nsteps = m // tq
    row = pl.BlockSpec((tq, d), lambda i, dest: (i, 0))
    vec = pl.BlockSpec((1, d), lambda i, dest: (0, 0))
    return pl.pallas_call(
        functools.partial(_combine_ln_kernel, alpha=alpha, tq=tq, nsteps=nsteps),
        grid_spec=pltpu.PrefetchScalarGridSpec(
            num_scalar_prefetch=1,
            grid=(nsteps,),
            in_specs=[row, pl.BlockSpec((tq, LANE), lambda i, dest: (i, 0)), vec, vec,
                      pl.BlockSpec(memory_space=pl.ANY)],
            out_specs=[row, row],
            scratch_shapes=[pltpu.VMEM((2, TOP_K, tq, d), F32), pltpu.SemaphoreType.DMA((2,))],
        ),
        out_shape=[jax.ShapeDtypeStruct((m, d), F32), jax.ShapeDtypeStruct((m, d), BF16)],
        compiler_params=_cparams(("arbitrary",)),
        name="moe_combine_ln",
    )(dest, x, top2, g.reshape(1, d), b.reshape(1, d), yg)


def _dot_nt(a, b):
    return lax.dot_general(a, b, (((1,), (1,)), ((), ())), preferred_element_type=F32)


def _dot_tn(a, b):
    return lax.dot_general(a, b, (((0,), (0,)), ((), ())), preferred_element_type=F32)


def _dot_split_lhs(stat2, x):
    hi, lo = _split2(x)
    return _dot(stat2, jnp.concatenate([hi, lo], axis=0))


def _dot_split_rhs(x, stat):
    hi, lo = _split2(x)
    return _dot(hi, stat) + _dot(lo, stat)


def _log_sigmoid(z):
    return jnp.minimum(z, 0.0) - jnp.log(1.0 + jnp.exp(-jnp.abs(z)))


def _softplus(z):
    return jnp.maximum(z, 0.0) + jnp.log(1.0 + jnp.exp(-jnp.abs(z)))


def _side_cast(side_in_ref, side_out_ref):
    cols = side_in_ref.shape[-1]
    side_out_ref[:, :cols] = side_in_ref[...].astype(BF16)
    if side_out_ref.shape[-1] > cols:
        side_out_ref[:, cols:] = jnp.zeros((side_out_ref.shape[0], side_out_ref.shape[-1] - cols), BF16)


def _side_plan(side, nsteps, index_map):
    w, out_cols = side
    cols = w.shape[-1]
    slab = (w.size // cols) // nsteps
    w3 = w.reshape(nsteps, slab, cols)
    in_spec = pl.BlockSpec((None, slab, cols), index_map)
    out_spec = pl.BlockSpec((None, slab, out_cols), index_map)
    return w3, in_spec, out_spec, jax.ShapeDtypeStruct((nsteps, slab, out_cols), BF16)


def _side_result(res, side):
    w, out_cols = side
    return res[1], res[0].reshape(w.shape[:-1] + (out_cols,))


def _n_levels(c):
    return int(round(math.log2(c)))


def _group_rows(gi, ngroups, unroll, rev):
    gg = (ngroups - 1 - gi) if rev else gi
    order = range(unroll - 1, -1, -1) if rev else range(unroll)
    return [pl.ds(pl.multiple_of((gg * unroll + u) * CHUNK, CHUNK), CHUNK) for u in order]


@functools.lru_cache(maxsize=None)
def _hg_static(c, rev):
    n = _n_levels(c)
    t = np.arange(c)[:, None]
    r = np.arange(c)[None, :]
    mats = [r <= t]
    d_mats, e_mats, masks = [], [], [t == r]
    for lvl in range(1, n + 1):
        blk, half = 1 << lvl, 1 << (lvl - 1)
        b0 = (t // blk) * blk
        mid = b0 + half - 1
        second = (t - b0) >= half
        d_mats.append(second & (r > mid) & (r <= t))
        e_mats.append((~second) & (r > t) & (r <= mid))
        masks.append((b0 == (r // blk) * blk) & second & ((r % blk) < half))
    mats = mats + d_mats + e_mats + [r > t]
    if rev:
        mats = [m[::-1, ::-1] for m in mats]
        masks = [m[::-1, ::-1] for m in masks]
    stat = np.concatenate(mats, axis=0).astype(np.float32)
    return stat, np.stack(masks).astype(np.float32)


def _hgrn2_kernel(*refs, rev, nchunk, unroll, epilogue, scale, side):
    if side:
        side_in_ref, *rest = refs
        n_in = 10 if epilogue else 7
        _side_cast(side_in_ref, rest[n_in])
        refs = rest[:n_in] + rest[n_in + 1:]
    if epilogue:
        (q_ref, z_ref, v_ref, llb_ref, l1m_ref, stat_ref, mask_ref,
         of_ref, og_ref, ng_ref, o_ref, st_ref) = refs
    else:
        q_ref, z_ref, v_ref, llb_ref, l1m_ref, stat_ref, mask_ref, o_ref, st_ref = refs
    c = CHUNK
    n = _n_levels(c)
    ngroups = nchunk // unroll

    @pl.when(pl.program_id(2) == 0)
    def _():
        st_ref[...] = jnp.zeros_like(st_ref)

    log_lb = llb_ref[...]
    log_1m_lb = l1m_ref[...]

    def group_step(gi, carry):
        rows_list = _group_rows(gi, ngroups, unroll, rev)
        nu = len(rows_list)
        q = [q_ref[rows, :] * scale for rows in rows_list]
        vb = [v_ref[rows, :].astype(BF16) for rows in rows_list]
        b = [log_1m_lb + _log_sigmoid(z_ref[rows, :]) for rows in rows_list]
        g = [jnp.maximum(log_lb, bi) + jnp.log(1.0 + jnp.exp(-jnp.abs(log_lb - bi))) for bi in b]
        k = [1.0 - jnp.exp(gi_) for gi_ in g]
        xall = _dot_split_lhs(stat_ref[...], jnp.concatenate(g, axis=1))

        def expo(blk, i):
            return xall[blk * c:(blk + 1) * c, i * LANE:(i + 1) * LANE]

        att = [mask_ref[0] * _dot_nt(q[i].astype(BF16), k[i].astype(BF16)) for i in range(nu)]
        for lvl in range(1, n + 1):
            ql = [(q[i] * jnp.exp(expo(lvl, i))).astype(BF16) for i in range(nu)]
            kl = [(k[i] * jnp.exp(expo(n + lvl, i))).astype(BF16) for i in range(nu)]
            att = [att[i] + mask_ref[lvl] * _dot_nt(ql[i], kl[i]) for i in range(nu)]
        o_intra = [_dot(att[i].astype(BF16), vb[i]) for i in range(nu)]
        qd = [(q[i] * jnp.exp(expo(0, i))).astype(BF16) for i in range(nu)]
        kd = [(k[i] * jnp.exp(expo(2 * n + 1, i))).astype(BF16) for i in range(nu)]
        dst = [_dot_tn(vb[i], kd[i]) for i in range(nu)]
        edge_row = 0 if rev else c - 1
        de = [jnp.exp(expo(0, i)[edge_row:edge_row + 1]) for i in range(nu)]
        states = [st_ref[...]]
        for i in range(nu):
            states.append(states[i] * de[i] + dst[i])
        st_ref[...] = states[nu]
        outs = [o_intra[i] + _dot_nt(qd[i], states[i].astype(BF16)) for i in range(nu)]
        for rows, o in zip(rows_list, outs):
            if epilogue:
                tot = o + of_ref[rows, :]
                y = tot * lax.rsqrt(jnp.mean(tot * tot, axis=-1, keepdims=True) + RMS_EPS) * ng_ref[...]
                gate = og_ref[rows, :]
                o_ref[rows, :] = (y * gate * _sigmoid(gate)).astype(o_ref.dtype)
            else:
                o_ref[rows, :] = o
        return carry

    lax.fori_loop(0, ngroups, group_step, 0)


def _hgrn2(p3, o_fwd, llb, l1m, norm_g, *, rev, heads, col, d_model, rows=2048, unroll=8, side=None):
    bsz, s, _ = p3.shape
    rows = _pick(s, rows)
    ns = s // rows
    nchunk = rows // CHUNK
    unroll = math.gcd(unroll, nchunk)
    stat, masks = _hg_static(CHUNK, rev)
    stat = jnp.asarray(np.concatenate([stat, stat], axis=1), BF16)
    masks = jnp.asarray(masks, F32)
    zcol = col["hg_f_bwd"] if rev else col["hg_f_fwd"]
    seq = (lambda i: ns - 1 - i) if rev else (lambda i: i)

    def pspec(c0):
        return pl.BlockSpec((None, rows, LANE), lambda b, h, i: (b, seq(i), c0 + h))

    vec = pl.BlockSpec((None, 1, LANE), lambda b, h, i: (h, 0, 0))
    in_specs = [pspec(col["hg_q"]), pspec(zcol), pspec(col["hg_i"]), vec, vec,
                pl.BlockSpec(stat.shape, lambda b, h, i: (0, 0)),
                pl.BlockSpec(masks.shape, lambda b, h, i: (0, 0, 0))]
    args = [p3, p3, p3, llb, l1m, stat, masks]
    if rev:
        in_specs += [pl.BlockSpec((None, rows, LANE), lambda b, h, i: (b, seq(i), h)),
                     pspec(col["hg_g"]),
                     pl.BlockSpec((1, LANE), lambda b, h, i: (0, 0))]
        args += [o_fwd, p3, norm_g]
    out_shape = jax.ShapeDtypeStruct((bsz, s, heads * LANE), BF16 if rev else F32)
    out_specs = pl.BlockSpec((None, rows, LANE), lambda b, h, i: (b, seq(i), h))
    if side is not None:
        side3, side_in, side_out, side_shape = _side_plan(
            side, bsz * heads * ns, lambda b, h, i: ((b * heads + h) * ns + i, 0, 0))
        in_specs, args = [side_in] + in_specs, [side3] + args
        out_specs, out_shape = [side_out, out_specs], [side_shape, out_shape]
    res = pl.pallas_call(
        functools.partial(_hgrn2_kernel, rev=rev, nchunk=nchunk, unroll=unroll, epilogue=rev,
                          scale=HEAD_DIM ** -0.5, side=side is not None),
        grid=(bsz, heads, ns),
        in_specs=in_specs,
        out_specs=out_specs,
        out_shape=out_shape,
        scratch_shapes=[pltpu.VMEM((HEAD_DIM, HEAD_DIM), F32)],
        compiler_params=_cparams(("parallel", "parallel", "arbitrary")),
        name="hgrn2_bwd" if rev else "hgrn2_fwd",
    )(*args)
    return (res, None) if side is None else _side_result(res, side)


def _gd_prep_kernel(x_ref, w_ref, o_ref, xp_ref, *, s, tile, n_qk, n_q, scale_q):
    j = pl.program_id(1)
    halo = 8
    xp_ref[0:halo, :] = jnp.zeros((halo, LANE), F32)
    xp_ref[halo + s:2 * halo + s, :] = jnp.zeros((halo, LANE), F32)
    xp_ref[halo:halo + s, :] = x_ref[...]
    w = w_ref[...]
    inv_scale = jnp.where(j < n_q, scale_q, 1.0)

    def tile_step(ti, carry):
        r0 = pl.multiple_of(ti * tile, tile)
        win = xp_ref[pl.ds(r0, tile + 2 * halo), :]
        acc = jnp.zeros((tile, LANE), F32)
        for tap in range(CONV_K):
            off = halo + tap - CONV_K // 2
            acc = acc + w[tap:tap + 1, :] * win[off:off + tile, :]
        y = acc * _sigmoid(acc)
        inv = lax.rsqrt(jnp.sum(y * y, axis=-1, keepdims=True) + RMS_EPS) * inv_scale
        o_ref[pl.ds(r0, tile), :] = y * jnp.where(j < n_qk, inv, 1.0)
        return carry

    lax.fori_loop(0, s // tile, tile_step, 0)


def _gd_prep(p3, conv_w, *, col0, heads, tile=1024):
    bsz, s, _ = p3.shape
    tile = _pick(s, tile)
    nblk = 3 * heads
    return pl.pallas_call(
        functools.partial(_gd_prep_kernel, s=s, tile=tile, n_qk=2 * heads, n_q=heads, scale_q=HEAD_DIM ** -0.5),
        grid=(bsz, nblk),
        in_specs=[pl.BlockSpec((None, s, LANE), lambda b, j: (b, 0, col0 + j)),
                  pl.BlockSpec((CONV_K, LANE), lambda b, j: (0, j))],
        out_specs=pl.BlockSpec((None, s, LANE), lambda b, j: (b, 0, j)),
        out_shape=jax.ShapeDtypeStruct((bsz, s, nblk * LANE), F32),
        scratch_shapes=[pltpu.VMEM((s + 16, LANE), F32)],
        compiler_params=_cparams(("parallel", "parallel")),
        name="gd_prep",
    )(p3, conv_w)


@functools.lru_cache(maxsize=None)
def _gd_static(c, rev):
    t = np.arange(c)[:, None]
    s = np.arange(c)[None, :]
    incl = (s >= t) if rev else (s <= t)
    strict = (s > t) if rev else (s < t)
    mats = [incl, strict, t == s, np.ones((c, c), bool), strict & ((t // 8) == (s // 8))]
    blk = 16
    while blk <= c:
        half = blk // 2
        same = (t // blk) == (s // blk)
        t2, s2 = (t % blk) >= half, (s % blk) >= half
        mats.append(same & ((~t2) & s2 if rev else t2 & (~s2)))
        blk *= 2
    return np.stack(mats).astype(np.float32)


def _gdn_kernel(*refs, rev, nchunk, unroll, epilogue, heads, hb, side):
    if side:
        side_in_ref, *rest = refs
        n_in = 11 if epilogue else 8
        _side_cast(side_in_ref, rest[n_in])
        refs = rest[:n_in] + rest[n_in + 1:]
    if epilogue:
        (q_ref, k_ref, v_ref, pg_ref, na_ref, dtb_ref, stat_ref, stat2_ref,
         of_ref, og_ref, ng_ref, o_ref, s_ref, gb_ref, bb_ref) = refs
    else:
        q_ref, k_ref, v_ref, pg_ref, na_ref, dtb_ref, stat_ref, stat2_ref, o_ref, s_ref, gb_ref, bb_ref = refs
    c = CHUNK
    ngroups = nchunk // unroll
    hg = pl.program_id(1)

    @pl.when(pl.program_id(2) == 0)
    def _():
        s_ref[...] = jnp.zeros_like(s_ref)

    incl = stat_ref[0]
    strict = stat_ref[1]
    eye = stat_ref[2]
    incl2 = stat2_ref[0]
    ones2 = stat2_ref[1]
    n_lvl = stat_ref.shape[0] - 5

    pg = pg_ref[...]
    glog = na_ref[...] * _softplus(pg + dtb_ref[...])
    beta = _sigmoid(pg)
    cum_all = jnp.concatenate(
        [_dot_split_lhs(incl2, glog[ci * c:(ci + 1) * c]) for ci in range(nchunk)], axis=0)
    rsel = lax.broadcasted_iota(jnp.int32, (LANE, LANE), 0)
    d = 1 if rev else 0
    for hh in range(hb):
        h = hg * hb + hh
        gb_ref[hh] = _dot_split_rhs(cum_all, jnp.where(rsel == d * heads + h, 1.0, 0.0).astype(BF16))
        bb_ref[hh] = _dot(beta.astype(BF16), jnp.where(rsel == (2 + d) * heads + h, 1.0, 0.0).astype(BF16))

    d8 = stat_ref[4]
    edge_row = 0 if rev else c - 1

    def group_step(gi, carry):
        rows_list = _group_rows(gi, ngroups, unroll, rev)
        items = [(rows, hh) for rows in rows_list for hh in range(hb)]
        ni = len(items)
        rng = range(ni)
        lanes = [slice(hh * LANE, (hh + 1) * LANE) for _, hh in items]
        q = [q_ref[rows, lanes[i]] for i, (rows, _) in enumerate(items)]
        k = [k_ref[rows, lanes[i]] for i, (rows, _) in enumerate(items)]
        v = [v_ref[rows, lanes[i]] for i, (rows, _) in enumerate(items)]
        cum = [gb_ref[hh, rows, :] for rows, hh in items]
        bb = [bb_ref[hh, rows, :] for rows, hh in items]
        cum_row = [_dot_split_lhs(ones2, cum[i][:, 0:c] * eye) for i in rng]
        decay = [jnp.exp(jnp.where(incl > 0, cum[i][:, 0:c] - cum_row[i], -jnp.inf)) for i in rng]
        ecum = [jnp.exp(cum[i]) for i in rng]
        kb = [k[i] * bb[i] for i in rng]
        kbf = [k[i].astype(BF16) for i in rng]
        a = [strict * _dot_nt(kb[i].astype(BF16), kbf[i]) * decay[i] for i in rng]
        qk = [(_dot_nt(q[i].astype(BF16), kbf[i]) * decay[i]).astype(BF16) for i in rng]
        a0 = [a[i] * d8 for i in rng]
        a0b = [a0[i].astype(BF16) for i in rng]
        a2b = [_dot(a0b[i], a0b[i]).astype(BF16) for i in rng]
        xm = [eye - a0[i] for i in rng]
        ym = [xm[i] + _dot(xm[i].astype(BF16), a2b[i]) for i in rng]
        a4b = [_dot(a2b[i], a2b[i]).astype(BF16) for i in rng]
        tm = [ym[i] + _dot(ym[i].astype(BF16), a4b[i]) for i in rng]
        for lvl in range(n_lvl):
            off = stat_ref[5 + lvl]
            tmb = [tm[i].astype(BF16) for i in rng]
            mid = [_dot(tmb[i], (a[i] * off).astype(BF16)).astype(BF16) for i in rng]
            tm = [tm[i] - _dot(mid[i], tmb[i]) for i in rng]
        tmb = [tm[i].astype(BF16) for i in rng]
        u = [_dot(tmb[i], (v[i] * bb[i]).astype(BF16)) for i in rng]
        w = [_dot(tmb[i], (kb[i] * ecum[i]).astype(BF16)).astype(BF16) for i in rng]
        edge = [cum[i][edge_row:edge_row + 1] for i in rng]
        qd = [(q[i] * ecum[i]).astype(BF16) for i in rng]
        kd = [(k[i] * jnp.exp(edge[i] - cum[i])).astype(BF16) for i in rng]
        de = [jnp.exp(edge[i]) for i in rng]
        st = [s_ref[hh] for hh in range(hb)]
        heads_rng = range(hb)
        for ui, rows in enumerate(rows_list):
            idx = [ui * hb + hh for hh in heads_rng]
            stb = [st[hh].astype(BF16) for hh in heads_rng]
            ws = [_dot(w[idx[hh]], stb[hh]) for hh in heads_rng]
            oq = [_dot(qd[idx[hh]], stb[hh]) for hh in heads_rng]
            vnb = [(u[idx[hh]] - ws[hh]).astype(BF16) for hh in heads_rng]
            dst = [_dot_tn(kd[idx[hh]], vnb[hh]) for hh in heads_rng]
            o = [oq[hh] + _dot(qk[idx[hh]], vnb[hh]) for hh in heads_rng]
            st = [st[hh] * de[idx[hh]] + dst[hh] for hh in heads_rng]
            for hh in heads_rng:
                ln = slice(hh * LANE, (hh + 1) * LANE)
                if epilogue:
                    tot = o[hh] + of_ref[rows, ln]
                    y = tot * lax.rsqrt(jnp.mean(tot * tot, axis=-1, keepdims=True) + RMS_EPS) * ng_ref[...]
                    gate = og_ref[rows, ln]
                    o_ref[rows, ln] = (y * gate * _sigmoid(gate)).astype(o_ref.dtype)
                else:
                    o_ref[rows, ln] = o[hh]
        for hh in heads_rng:
            s_ref[hh] = st[hh]
        return carry

    lax.fori_loop(0, ngroups, group_step, 0)


def _gdn(qkv, pg, p3, o_fwd, neg_a, dtb, norm_g, *, rev, heads, gcol, rows=512, unroll=4, hb=8, side=None):
    bsz, s, _ = qkv.shape
    rows = _pick(s, rows)
    ns = s // rows
    nchunk = rows // CHUNK
    unroll = math.gcd(unroll, nchunk)
    hb = math.gcd(hb, heads)
    stat_np = _gd_static(CHUNK, rev)
    stat = jnp.asarray(stat_np, F32)
    stat2 = jnp.asarray(np.stack([np.concatenate([stat_np[i], stat_np[i]], axis=1) for i in (0, 3)]), BF16)
    seq = (lambda i: ns - 1 - i) if rev else (lambda i: i)
    wide = hb * LANE

    def qspec(c0):
        return pl.BlockSpec((None, rows, wide), lambda b, h, i: (b, seq(i), c0 // hb + h))

    vec = pl.BlockSpec((1, LANE), lambda b, h, i: (0, 0))
    in_specs = [qspec(0), qspec(heads), qspec(2 * heads),
                pl.BlockSpec((None, rows, LANE), lambda b, h, i: (b, seq(i), 0)),
                vec, vec, pl.BlockSpec(stat.shape, lambda b, h, i: (0, 0, 0)),
                pl.BlockSpec(stat2.shape, lambda b, h, i: (0, 0, 0))]
    args = [qkv, qkv, qkv, pg, neg_a, dtb, stat, stat2]
    if rev:
        in_specs += [qspec(0), qspec(gcol), vec]
        args += [o_fwd, p3, norm_g]
    out_shape = jax.ShapeDtypeStruct((bsz, s, heads * LANE), BF16 if rev else F32)
    out_spec = qspec(0)
    if side is not None:
        ng = heads // hb
        side3, side_in, side_out, side_shape = _side_plan(
            side, bsz * ng * ns, lambda b, h, i: ((b * ng + h) * ns + i, 0, 0))
        in_specs, args = [side_in] + in_specs, [side3] + args
        out_spec, out_shape = [side_out, out_spec], [side_shape, out_shape]
    res = pl.pallas_call(
        functools.partial(_gdn_kernel, rev=rev, nchunk=nchunk, unroll=unroll, epilogue=rev, heads=heads, hb=hb,
                          side=side is not None),
        grid=(bsz, heads // hb, ns),
        in_specs=in_specs,
        out_specs=out_spec,
        out_shape=out_shape,
        scratch_shapes=[pltpu.VMEM((hb, HEAD_DIM, HEAD_DIM), F32),
                        pltpu.VMEM((hb, rows, LANE), F32), pltpu.VMEM((hb, rows, LANE), F32)],
        compiler_params=_cparams(("parallel", "parallel", "arbitrary")),
        name="gdn_bwd" if rev else "gdn_fwd",
    )(*args)
    return (res, None) if side is None else _side_result(res, side)


def _pad_cols(w, n):
    return jnp.pad(w, ((0, 0),) * (w.ndim - 1) + ((0, n - w.shape[-1]),))


def kernel(x, ln_in_g, ln_in_b, w_in, hg_lb_param, gd_conv_w, gd_a_log, gd_dt_bias, hg_norm_g, gd_norm_g, w_out,
           ln1_g, ln1_b, ffn_w_gate, ffn_w_up, ffn_w_down, moe_router, moe_w_gate, moe_w_up, moe_w_down,
           ln2_g, ln2_b):
    bsz, s, d = x.shape
    depth = w_in.shape[0]
    m = bsz * s
    half = d // 2
    heads = half // HEAD_DIM
    alpha = (2 * depth) ** 0.25
    n_main = 8 * half
    n_gate = N_GATE_COLS * heads
    names = ("hg_q", "hg_f_fwd", "hg_f_bwd", "hg_i", "hg_g", "gd_q", "gd_k", "gd_v", "gd_g")
    col = {nm: i * heads for i, nm in enumerate(names)}

    lb = jnp.cumsum(jax.nn.softmax(hg_lb_param.astype(F32), axis=0), axis=0)
    lb = lb - lb[0:1]
    log_lb = jnp.log(lb).reshape(depth, 2, heads, 1, LANE)
    log_1m_lb = jnp.log1p(-lb).reshape(depth, 2, heads, 1, LANE)

    f_dense = ffn_w_gate.shape[-1]
    f_pad = -(-f_dense // 1024) * 1024
    tasks = []
    for l in range(depth):
        due = 3 * l + 2
        tasks.append((due, ("w_out", l), w_out[l], d))
        if l % 2 == 0:
            tasks += [(due, ("ffn_gate", l // 2), ffn_w_gate[l // 2], f_pad),
                      (due, ("ffn_up", l // 2), ffn_w_up[l // 2], f_pad)]
        else:
            tasks += [(due, (nm, l // 2), wt[l // 2], wt.shape[-1])
                      for nm, wt in (("moe_gate", moe_w_gate), ("moe_up", moe_w_up), ("moe_down", moe_w_down))]
    tasks.sort(key=lambda t: (t[0], -t[2].size))
    cast_bf16 = {}

    def ride_along(carrier, nsteps):
        for t in tasks:
            due, _, w, _ = t
            if carrier <= due and (w.size // w.shape[-1]) % (16 * nsteps) == 0:
                tasks.remove(t)
                return t
        return None

    def carried(fn, carrier, nsteps, *args, **kwargs):
        t = ride_along(carrier, nsteps)
        if t is None:
            return fn(*args, **kwargs)[0]
        out, cast_bf16[t[1]] = fn(*args, side=(t[2], t[3]), **kwargs)
        return out

    def bf16_weight(key, w, out_cols):
        if key in cast_bf16:
            return cast_bf16[key]
        return _pad_cols(w, out_cols).astype(BF16)

    hg_steps = bsz * heads * (s // _pick(s, 2048))
    gd_steps = bsz * (heads // math.gcd(8, heads)) * (s // _pick(s, 512))

    xf, xb = _layer_norm(x.reshape(m, d), None, ln_in_g, ln_in_b)
    for l in range(depth):
        w_l = w_in[l]
        w_main = jnp.concatenate([w_l[:, :n_main], w_l[:, n_main + n_gate:]], axis=1).astype(BF16)
        w_gate = _pad_cols(w_l[:, n_main:n_main + n_gate], LANE).astype(BF16)
        p3 = _matmul(xb, w_main, F32, tn=1024, name="in_proj").reshape(bsz, s, -1)
        pg = _matmul(xb, w_gate, F32, tn=LANE, name="gate_proj").reshape(bsz, s, LANE)

        o_f = carried(_hgrn2, 3 * l, hg_steps, p3, None, log_lb[l, 0], log_1m_lb[l, 0], None,
                      rev=False, heads=heads, col=col, d_model=d)
        hg_out = carried(_hgrn2, 3 * l + 1, hg_steps, p3, o_f, log_lb[l, 1], log_1m_lb[l, 1],
                         hg_norm_g[l].reshape(1, LANE).astype(F32), rev=True, heads=heads, col=col, d_model=d)

        qkv = _gd_prep(p3, gd_conv_w[l].astype(F32), col0=col["gd_q"], heads=heads)
        neg_a = jnp.zeros((1, LANE), F32).at[0, :2 * heads].set(-jnp.exp(gd_a_log[l].astype(F32)).reshape(-1))
        dtb = jnp.zeros((1, LANE), F32).at[0, :2 * heads].set(gd_dt_bias[l].astype(F32).reshape(-1))
        gn = gd_norm_g[l].reshape(1, LANE).astype(F32)
        g_f = carried(_gdn, 3 * l + 2, gd_steps, qkv, pg, None, None, neg_a, dtb, None,
                      rev=False, heads=heads, gcol=col["gd_g"])
        gd_out = _gdn(qkv, pg, p3, g_f, neg_a, dtb, gn, rev=True, heads=heads, gcol=col["gd_g"])[0]

        mix = _matmul_cat(hg_out.reshape(m, half), gd_out.reshape(m, half), bf16_weight(("w_out", l), w_out[l], d),
                          F32, name="out_proj")
        xf, xb = _layer_norm(xf, mix, ln1_g[l], ln1_b[l], alpha=alpha)

        j = l // 2
        if l % 2 == 0:
            hdn = _swiglu(xb, bf16_weight(("ffn_gate", j), ffn_w_gate[j], f_pad),
                          bf16_weight(("ffn_up", j), ffn_w_up[j], f_pad))
            wd = jnp.pad(ffn_w_down[j], ((0, f_pad - f_dense), (0, 0))).astype(BF16)
            ffn = _matmul(hdn, wd, F32, tn=1024, tk=2816, name="down_proj")
            xf, xb = _layer_norm(xf, ffn, ln2_g[l], ln2_b[l], alpha=alpha)
        else:
            n_experts = moe_router.shape[-1]
            top2 = _router_top2(xf, moe_router[j])
            dest, tile_expert, n_used, n_tiles = _route_plan(top2, n_experts, MOE_TILE)
            xg = _dispatch(xf, dest, n_tiles * MOE_TILE)
            wts = [bf16_weight((nm, j), wt[j], wt.shape[-1])
                   for nm, wt in (("moe_gate", moe_w_gate), ("moe_up", moe_w_up), ("moe_down", moe_w_down))]
            yg = _grouped_experts(xg, *wts, tile_expert, n_used)
            xf, xb = _combine_layer_norm(xf, yg, top2, dest, ln2_g[l], ln2_b[l], alpha)
    return xf.reshape(bsz, s, d)
```

```python
import functools
import math

import numpy as np
import jax
import jax.numpy as jnp
from jax import lax
from jax.experimental import pallas as pl
from jax.experimental.pallas import tpu as pltpu

F32 = jnp.float32
BF16 = jnp.bfloat16

LANE = 128
HEAD_DIM = 128
CHUNK = 64
CONV_K = 5
N_GATE_COLS = 4
LN_EPS = 1e-5
RMS_EPS = 1e-6
VMEM_LIMIT_BYTES = 56 * 1024 * 1024


def _cparams(sem):
    return pltpu.CompilerParams(dimension_semantics=sem, vmem_limit_bytes=VMEM_LIMIT_BYTES)


def _pick(n, pref):
    t = min(n, pref)
    while n % t:
        t -= LANE
    return t


def _mm_kernel(a_ref, b_ref, o_ref):
    o_ref[...] = jnp.dot(a_ref[...], b_ref[...], preferred_element_type=F32).astype(o_ref.dtype)


def _mm_acc_kernel(a_ref, b_ref, o_ref, acc_ref, *, nk):
    k = pl.program_id(2)
    prod = jnp.dot(a_ref[...], b_ref[...], preferred_element_type=F32)

    @pl.when(k == 0)
    def _():
        acc_ref[...] = prod

    @pl.when(k > 0)
    def _():
        acc_ref[...] += prod

    @pl.when(k == nk - 1)
    def _():
        o_ref[...] = acc_ref[...].astype(o_ref.dtype)


def _matmul(a, b, out_dtype, tm=1024, tn=512, tk=None, name="matmul"):
    m, kdim = a.shape
    n = b.shape[1]
    tm, tn = _pick(m, tm), _pick(n, tn)
    tk = kdim if tk is None else _pick(kdim, tk)
    nk = kdim // tk
    if nk == 1:
        return pl.pallas_call(
            _mm_kernel,
            grid=(m // tm, n // tn),
            in_specs=[pl.BlockSpec((tm, kdim), lambda i, j: (i, 0)),
                      pl.BlockSpec((kdim, tn), lambda i, j: (0, j))],
            out_specs=pl.BlockSpec((tm, tn), lambda i, j: (i, j)),
            out_shape=jax.ShapeDtypeStruct((m, n), out_dtype),
            compiler_params=_cparams(("parallel", "arbitrary")),
            name=name,
        )(a, b)
    return pl.pallas_call(
        functools.partial(_mm_acc_kernel, nk=nk),
        grid=(m // tm, n // tn, nk),
        in_specs=[pl.BlockSpec((tm, tk), lambda i, j, k: (i, k)),
                  pl.BlockSpec((tk, tn), lambda i, j, k: (k, j))],
        out_specs=pl.BlockSpec((tm, tn), lambda i, j, k: (i, j)),
        out_shape=jax.ShapeDtypeStruct((m, n), out_dtype),
        scratch_shapes=[pltpu.VMEM((tm, tn), F32)],
        compiler_params=_cparams(("parallel", "arbitrary", "arbitrary")),
        name=name,
    )(a, b)


def _mm2_kernel(a1_ref, a2_ref, b1_ref, b2_ref, o_ref):
    acc = jnp.dot(a1_ref[...], b1_ref[...], preferred_element_type=F32)
    acc = acc + jnp.dot(a2_ref[...], b2_ref[...], preferred_element_type=F32)
    o_ref[...] = acc.astype(o_ref.dtype)


def _matmul_cat(a1, a2, b, out_dtype, tm=1024, tn=512, name="matmul_cat"):
    m, kh = a1.shape
    n = b.shape[1]
    tm, tn = _pick(m, tm), _pick(n, tn)
    return pl.pallas_call(
        _mm2_kernel,
        grid=(m // tm, n // tn),
        in_specs=[pl.BlockSpec((tm, kh), lambda i, j: (i, 0)), pl.BlockSpec((tm, kh), lambda i, j: (i, 0)),
                  pl.BlockSpec((kh, tn), lambda i, j: (0, j)), pl.BlockSpec((kh, tn), lambda i, j: (1, j))],
        out_specs=pl.BlockSpec((tm, tn), lambda i, j: (i, j)),
        out_shape=jax.ShapeDtypeStruct((m, n), out_dtype),
        compiler_params=_cparams(("parallel", "arbitrary")),
        name=name,
    )(a1, a2, b, b)


def _sigmoid(x):
    return 1.0 / (1.0 + jnp.exp(-x))


def _dot(a, b):
    return jnp.dot(a, b, preferred_element_type=F32)


def _split2(x):
    hi = x.astype(BF16)
    return hi, (x - hi.astype(F32)).astype(BF16)


def _swiglu_kernel(a_ref, wg_ref, wu_ref, o_ref):
    a = a_ref[...]
    g = jnp.dot(a, wg_ref[...], preferred_element_type=F32)
    u = jnp.dot(a, wu_ref[...], preferred_element_type=F32)
    o_ref[...] = (g * _sigmoid(g) * u).astype(o_ref.dtype)


def _swiglu(a, wg, wu, tm=1024, tn=512, name="swiglu"):
    m, kdim = a.shape
    f = wg.shape[1]
    tm, tn = _pick(m, tm), _pick(f, tn)
    wspec = pl.BlockSpec((kdim, tn), lambda i, j: (0, j))
    return pl.pallas_call(
        _swiglu_kernel,
        grid=(m // tm, f // tn),
        in_specs=[pl.BlockSpec((tm, kdim), lambda i, j: (i, 0)), wspec, wspec],
        out_specs=pl.BlockSpec((tm, tn), lambda i, j: (i, j)),
        out_shape=jax.ShapeDtypeStruct((m, f), BF16),
        compiler_params=_cparams(("parallel", "arbitrary")),
        name=name,
    )(a, wg, wu)


def _ln_kernel(*refs, alpha, has_res):
    if has_res:
        x_ref, r_ref, g_ref, b_ref, of_ref, ob_ref = refs
        x = alpha * x_ref[...] + r_ref[...]
    else:
        x_ref, g_ref, b_ref, of_ref, ob_ref = refs
        x = x_ref[...]
    mu = jnp.mean(x, axis=-1, keepdims=True)
    xc = x - mu
    var = jnp.mean(xc * xc, axis=-1, keepdims=True)
    y = xc * lax.rsqrt(var + LN_EPS) * g_ref[...] + b_ref[...]
    of_ref[...] = y
    ob_ref[...] = y.astype(BF16)


def _layer_norm(x, res, g, b, alpha=1.0, tm=256):
    m, d = x.shape
    tm = _pick(m, tm)
    row = pl.BlockSpec((tm, d), lambda i: (i, 0))
    vec = pl.BlockSpec((1, d), lambda i: (0, 0))
    args = [x] + ([res] if res is not None else []) + [g.reshape(1, d), b.reshape(1, d)]
    return pl.pallas_call(
        functools.partial(_ln_kernel, alpha=alpha, has_res=res is not None),
        grid=(m // tm,),
        in_specs=[row] * (len(args) - 2) + [vec, vec],
        out_specs=[row, row],
        out_shape=[jax.ShapeDtypeStruct((m, d), F32), jax.ShapeDtypeStruct((m, d), BF16)],
        compiler_params=_cparams(("parallel",)),
        name="layer_norm",
    )(*args)


def _router_kernel(x_ref, r_ref, o_ref, *, n_experts):
    x_hi, x_lo = _split2(x_ref[...])
    r_hi, r_lo = _split2(r_ref[...])
    logits = _dot(x_hi, r_hi) + _dot(x_lo, r_hi) + _dot(x_hi, r_lo)
    lane = lax.broadcasted_iota(jnp.int32, logits.shape, 1).astype(F32)
    neg = -jnp.inf
    lg = jnp.where(lane < n_experts, logits, neg)
    m1 = jnp.max(lg, axis=1, keepdims=True)
    i1 = jnp.min(jnp.where(lg == m1, lane, float(LANE)), axis=1, keepdims=True)
    lg2 = jnp.where(lane == i1, neg, lg)
    m2 = jnp.max(lg2, axis=1, keepdims=True)
    i2 = jnp.min(jnp.where(lg2 == m2, lane, float(LANE)), axis=1, keepdims=True)
    e2 = jnp.exp(m2 - m1)
    w1 = 1.0 / (1.0 + e2)
    o_ref[...] = (jnp.where(lane == 0.0, i1, 0.0) + jnp.where(lane == 1.0, i2, 0.0)
                  + jnp.where(lane == 2.0, w1, 0.0) + jnp.where(lane == 3.0, e2 * w1, 0.0))


def _router_top2(x, router, tm=512):
    m, d = x.shape
    ne = router.shape[1]
    tm = _pick(m, tm)
    rpad = jnp.zeros((d, LANE), F32).at[:, :ne].set(router.astype(F32))
    return pl.pallas_call(
        functools.partial(_router_kernel, n_experts=ne),
        grid=(m // tm,),
        in_specs=[pl.BlockSpec((tm, d), lambda i: (i, 0)), pl.BlockSpec((d, LANE), lambda i: (0, 0))],
        out_specs=pl.BlockSpec((tm, LANE), lambda i: (i, 0)),
        out_shape=jax.ShapeDtypeStruct((m, LANE), F32),
        compiler_params=_cparams(("parallel",)),
        name="router",
    )(x, rpad)


MOE_TILE = 512
MOE_TOKENS = 256
TOP_K = 2


def _route_plan(top2, n_experts, tile):
    m = top2.shape[0]
    ef = top2[:, 0:TOP_K].astype(jnp.int32).reshape(-1)
    oh = (ef[:, None] == jnp.arange(n_experts, dtype=jnp.int32)[None, :]).astype(jnp.int32)
    rank = jnp.cumsum(oh, axis=0) - oh
    counts = jnp.sum(oh, axis=0)
    padded = ((counts + tile - 1) // tile) * tile
    ends = jnp.cumsum(padded)
    starts = ends - padded
    dest = jnp.sum(oh * (starts[None, :] + rank), axis=1).astype(jnp.int32)
    n_tiles = (TOP_K * m) // tile + n_experts
    tile_start = jnp.arange(n_tiles, dtype=jnp.int32) * tile
    tile_expert = jnp.sum((tile_start[:, None] >= ends[None, :]).astype(jnp.int32), axis=1)
    tile_expert = jnp.minimum(tile_expert, n_experts - 1).astype(jnp.int32)
    n_used = (ends[-1] // tile).astype(jnp.int32).reshape(1)
    return dest, tile_expert, n_used, n_tiles


def _dispatch_kernel(dest_ref, x_ref, _xg_in, xg_ref, sem, *, tq):
    base = pl.program_id(0) * tq

    def row_copy(r, slot):
        row = dest_ref[TOP_K * (base + r) + slot]
        return pltpu.make_async_copy(x_ref.at[pl.ds(r, 1)], xg_ref.at[pl.ds(row, 1)], sem)

    def issue(r, carry):
        for slot in range(TOP_K):
            row_copy(r, slot).start(priority=slot % 2)
        return carry

    lax.fori_loop(0, tq, issue, 0, unroll=4)
    for _ in range(TOP_K):
        pltpu.make_async_copy(x_ref, xg_ref.at[pl.ds(0, tq)], sem).wait()


def _dispatch(xf, dest, n_rows, tq=MOE_TOKENS):
    m, d = xf.shape
    tq = _pick(m, tq)
    return pl.pallas_call(
        functools.partial(_dispatch_kernel, tq=tq),
        grid_spec=pltpu.PrefetchScalarGridSpec(
            num_scalar_prefetch=1,
            grid=(m // tq,),
            in_specs=[pl.BlockSpec((tq, d), lambda i, dest: (i, 0)), pl.BlockSpec(memory_space=pl.ANY)],
            out_specs=pl.BlockSpec(memory_space=pl.ANY),
            scratch_shapes=[pltpu.SemaphoreType.DMA(())],
        ),
        out_shape=jax.ShapeDtypeStruct((n_rows, d), F32),
        input_output_aliases={2: 0},
        compiler_params=_cparams(("arbitrary",)),
        name="moe_dispatch",
    )(dest, xf, jnp.zeros((n_rows, d), F32))


def _gswiglu_kernel(te_ref, nu_ref, a_ref, wg_ref, wu_ref, o_ref, abf_ref):
    i = pl.program_id(0)

    @pl.when(pl.program_id(1) == 0)
    def _():
        abf_ref[...] = a_ref[...].astype(BF16)

    @pl.when(i < nu_ref[0])
    def _():
        a = abf_ref[...]
        g = jnp.dot(a, wg_ref[...], preferred_element_type=F32)
        u = jnp.dot(a, wu_ref[...], preferred_element_type=F32)
        o_ref[...] = (g * _sigmoid(g) * u).astype(o_ref.dtype)

    @pl.when(i >= nu_ref[0])
    def _():
        o_ref[...] = jnp.zeros_like(o_ref)


def _gdown_kernel(te_ref, nu_ref, a_ref, w_ref, o_ref):
    i = pl.program_id(0)

    @pl.when(i < nu_ref[0])
    def _():
        o_ref[...] = jnp.dot(a_ref[...], w_ref[...], preferred_element_type=F32)

    @pl.when(i >= nu_ref[0])
    def _():
        o_ref[...] = jnp.zeros_like(o_ref)


def _grouped_experts(xg, wg, wu, wd, tile_expert, n_used, tile=MOE_TILE, tn=512, tn_down=1024):
    p, d = xg.shape
    f = wg.shape[2]
    tn, tn_down = _pick(f, tn), _pick(d, tn_down)
    nt = p // tile

    def wspec(rows, cols):
        return pl.BlockSpec((None, rows, cols),
                            lambda i, j, te, nu: (te[i], 0, jnp.where(i < nu[0], j, 0)))

    hdn = pl.pallas_call(
        _gswiglu_kernel,
        grid_spec=pltpu.PrefetchScalarGridSpec(
            num_scalar_prefetch=2,
            grid=(nt, f // tn),
            in_specs=[pl.BlockSpec((tile, d), lambda i, j, te, nu: (i, 0)), wspec(d, tn), wspec(d, tn)],
            out_specs=pl.BlockSpec((tile, tn), lambda i, j, te, nu: (i, j)),
            scratch_shapes=[pltpu.VMEM((tile, d), BF16)],
        ),
        out_shape=jax.ShapeDtypeStruct((p, f), BF16),
        compiler_params=_cparams(("arbitrary", "arbitrary")),
        name="moe_swiglu",
    )(tile_expert, n_used, xg, wg, wu)
    return pl.pallas_call(
        _gdown_kernel,
        grid_spec=pltpu.PrefetchScalarGridSpec(
            num_scalar_prefetch=2,
            grid=(nt, d // tn_down),
            in_specs=[pl.BlockSpec((tile, f), lambda i, j, te, nu: (i, 0)), wspec(f, tn_down)],
            out_specs=pl.BlockSpec((tile, tn_down), lambda i, j, te, nu: (i, j)),
        ),
        out_shape=jax.ShapeDtypeStruct((p, d), F32),
        compiler_params=_cparams(("arbitrary", "arbitrary")),
        name="moe_down",
    )(tile_expert, n_used, hdn, wd)


def _combine_ln_kernel(dest_ref, x_ref, top2_ref, g_ref, b_ref, yg_ref, of_ref, ob_ref, buf_ref, sem,
                       *, alpha, tq, nsteps):
    i = pl.program_id(0)
    slot = i % 2

    def fetch(step, slot_):
        base = step * tq

        def issue(r, carry):
            for k in range(TOP_K):
                row = dest_ref[TOP_K * (base + r) + k]
                pltpu.make_async_copy(yg_ref.at[pl.ds(row, 1)], buf_ref.at[slot_, k, pl.ds(r, 1)],
                                      sem.at[slot_]).start(priority=k % 2)
            return carry

        lax.fori_loop(0, tq, issue, 0, unroll=4)

    @pl.when(i == 0)
    def _():
        fetch(i, slot)

    @pl.when(i + 1 < nsteps)
    def _():
        fetch(i + 1, 1 - slot)

    for k in range(TOP_K):
        pltpu.make_async_copy(yg_ref.at[pl.ds(0, tq)], buf_ref.at[slot, k], sem.at[slot]).wait()
    top2 = top2_ref[...]
    ffn = top2[:, TOP_K:TOP_K + 1] * buf_ref[slot, 0]
    for k in range(1, TOP_K):
        ffn = ffn + top2[:, TOP_K + k:TOP_K + k + 1] * buf_ref[slot, k]
    x = alpha * x_ref[...] + ffn
    mu = jnp.mean(x, axis=-1, keepdims=True)
    xc = x - mu
    var = jnp.mean(xc * xc, axis=-1, keepdims=True)
    y = xc * lax.rsqrt(var + LN_EPS) * g_ref[...] + b_ref[...]
    of_ref[...] = y
    ob_ref[...] = y.astype(BF16)


def _combine_layer_norm(x, yg, top2, dest, g, b, alpha, tq=MOE_TOKENS):
    m, d = x.shape
    tq = _pick(m, tq)
    nsteps = m // tq
    row = pl.BlockSpec((tq, d), lambda i, dest: (i, 0))
    vec = pl.BlockSpec((1, d), lambda i, dest: (0, 0))
    return pl.pallas_call(
        functools.partial(_combine_ln_kernel, alpha=alpha, tq=tq, nsteps=nsteps),
        grid_spec=pltpu.PrefetchScalarGridSpec(
            num_scalar_prefetch=1,
            grid=(nsteps,),
            in_specs=[row, pl.BlockSpec((tq, LANE), lambda i, dest: (i, 0)), vec, vec,
                      pl.BlockSpec(memory_space=pl.ANY)],
            out_specs=[row, row],
            scratch_shapes=[pltpu.VMEM((2, TOP_K, tq, d), F32), pltpu.SemaphoreType.DMA((2,))],
        ),
        out_shape=[jax.ShapeDtypeStruct((m, d), F32), jax.ShapeDtypeStruct((m, d), BF16)],
        compiler_params=_cparams(("arbitrary",)),
        name="moe_combine_ln",
    )(dest, x, top2, g.reshape(1, d), b.reshape(1, d), yg)


def _dot_nt(a, b):
    return lax.dot_general(a, b, (((1,), (1,)), ((), ())), preferred_element_type=F32)


def _dot_tn(a, b):
    return lax.dot_general(a, b, (((0,), (0,)), ((), ())), preferred_element_type=F32)


def _dot_split_lhs(stat2, x):
    hi, lo = _split2(x)
    return _dot(stat2, jnp.concatenate([hi, lo], axis=0))


def _dot_split_rhs(x, stat):
    hi, lo = _split2(x)
    return _dot(hi, stat) + _dot(lo, stat)


def _log_sigmoid(z):
    return jnp.minimum(z, 0.0) - jnp.log(1.0 + jnp.exp(-jnp.abs(z)))


def _softplus(z):
    return jnp.maximum(z, 0.0) + jnp.log(1.0 + jnp.exp(-jnp.abs(z)))


def _side_cast(side_in_ref, side_out_ref):
    cols = side_in_ref.shape[-1]
    side_out_ref[:, :cols] = side_in_ref[...].astype(BF16)
    if side_out_ref.shape[-1] > cols:
        side_out_ref[:, cols:] = jnp.zeros((side_out_ref.shape[0], side_out_ref.shape[-1] - cols), BF16)


def _side_plan(side, nsteps, index_map):
    w, out_cols = side
    cols = w.shape[-1]
    slab = (w.size // cols) // nsteps
    w3 = w.reshape(nsteps, slab, cols)
    in_spec = pl.BlockSpec((None, slab, cols), index_map)
    out_spec = pl.BlockSpec((None, slab, out_cols), index_map)
    return w3, in_spec, out_spec, jax.ShapeDtypeStruct((nsteps, slab, out_cols), BF16)


def _side_result(res, side):
    w, out_cols = side
    return res[1], res[0].reshape(w.shape[:-1] + (out_cols,))


def _n_levels(c):
    return int(round(math.log2(c)))


def _group_rows(gi, ngroups, unroll, rev):
    gg = (ngroups - 1 - gi) if rev else gi
    order = range(unroll - 1, -1, -1) if rev else range(unroll)
    return [pl.ds(pl.multiple_of((gg * unroll + u) * CHUNK, CHUNK), CHUNK) for u in order]


@functools.lru_cache(maxsize=None)
def _hg_static(c, rev):
    n = _n_levels(c)
    t = np.arange(c)[:, None]
    r = np.arange(c)[None, :]
    mats = [r <= t]
    d_mats, e_mats, masks = [], [], [t == r]
    for lvl in range(1, n + 1):
        blk, half = 1 << lvl, 1 << (lvl - 1)
        b0 = (t // blk) * blk
        mid = b0 + half - 1
        second = (t - b0) >= half
        d_mats.append(second & (r > mid) & (r <= t))
        e_mats.append((~second) & (r > t) & (r <= mid))
        masks.append((b0 == (r // blk) * blk) & second & ((r % blk) < half))
    mats = mats + d_mats + e_mats + [r > t]
    if rev:
        mats = [m[::-1, ::-1] for m in mats]
        masks = [m[::-1, ::-1] for m in masks]
    stat = np.concatenate(mats, axis=0).astype(np.float32)
    return stat, np.stack(masks).astype(np.float32)


def _hgrn2_kernel(*refs, rev, nchunk, unroll, epilogue, scale, side):
    if side:
        side_in_ref, *rest = refs
        n_in = 10 if epilogue else 7
        _side_cast(side_in_ref, rest[n_in])
        refs = rest[:n_in] + rest[n_in + 1:]
    if epilogue:
        (q_ref, z_ref, v_ref, llb_ref, l1m_ref, stat_ref, mask_ref,
         of_ref, og_ref, ng_ref, o_ref, st_ref) = refs
    else:
        q_ref, z_ref, v_ref, llb_ref, l1m_ref, stat_ref, mask_ref, o_ref, st_ref = refs
    c = CHUNK
    n = _n_levels(c)
    ngroups = nchunk // unroll

    @pl.when(pl.program_id(2) == 0)
    def _():
        st_ref[...] = jnp.zeros_like(st_ref)

    log_lb = llb_ref[...]
    log_1m_lb = l1m_ref[...]

    def group_step(gi, carry):
        rows_list = _group_rows(gi, ngroups, unroll, rev)
        nu = len(rows_list)
        q = [q_ref[rows, :] * scale for rows in rows_list]
        vb = [v_ref[rows, :].astype(BF16) for rows in rows_list]
        b = [log_1m_lb + _log_sigmoid(z_ref[rows, :]) for rows in rows_list]
        g = [jnp.maximum(log_lb, bi) + jnp.log(1.0 + jnp.exp(-jnp.abs(log_lb - bi))) for bi in b]
        k = [1.0 - jnp.exp(gi_) for gi_ in g]
        xall = _dot_split_lhs(stat_ref[...], jnp.concatenate(g, axis=1))

        def expo(blk, i):
            return xall[blk * c:(blk + 1) * c, i * LANE:(i + 1) * LANE]

        att = [mask_ref[0] * _dot_nt(q[i].astype(BF16), k[i].astype(BF16)) for i in range(nu)]
        for lvl in range(1, n + 1):
            ql = [(q[i] * jnp.exp(expo(lvl, i))).astype(BF16) for i in range(nu)]
            kl = [(k[i] * jnp.exp(expo(n + lvl, i))).astype(BF16) for i in range(nu)]
            att = [att[i] + mask_ref[lvl] * _dot_nt(ql[i], kl[i]) for i in range(nu)]
        o_intra = [_dot(att[i].astype(BF16), vb[i]) for i in range(nu)]
        qd = [(q[i] * jnp.exp(expo(0, i))).astype(BF16) for i in range(nu)]
        kd = [(k[i] * jnp.exp(expo(2 * n + 1, i))).astype(BF16) for i in range(nu)]
        dst = [_dot_tn(vb[i], kd[i]) for i in range(nu)]
        edge_row = 0 if rev else c - 1
        de = [jnp.exp(expo(0, i)[edge_row:edge_row + 1]) for i in range(nu)]
        states = [st_ref[...]]
        for i in range(nu):
            states.append(states[i] * de[i] + dst[i])
        st_ref[...] = states[nu]
        outs = [o_intra[i] + _dot_nt(qd[i], states[i].astype(BF16)) for i in range(nu)]
        for rows, o in zip(rows_list, outs):
            if epilogue:
                tot = o + of_ref[rows, :]
                y = tot * lax.rsqrt(jnp.mean(tot * tot, axis=-1, keepdims=True) + RMS_EPS) * ng_ref[...]
                gate = og_ref[rows, :]
                o_ref[rows, :] = (y * gate * _sigmoid(gate)).astype(o_ref.dtype)
            else:
                o_ref[rows, :] = o
        return carry

    lax.fori_loop(0, ngroups, group_step, 0)


def _hgrn2(p3, o_fwd, llb, l1m, norm_g, *, rev, heads, col, d_model, rows=2048, unroll=8, side=None):
    bsz, s, _ = p3.shape
    rows = _pick(s, rows)
    ns = s // rows
    nchunk = rows // CHUNK
    unroll = math.gcd(unroll, nchunk)
    stat, masks = _hg_static(CHUNK, rev)
    stat = jnp.asarray(np.concatenate([stat, stat], axis=1), BF16)
    masks = jnp.asarray(masks, F32)
    zcol = col["hg_f_bwd"] if rev else col["hg_f_fwd"]
    seq = (lambda i: ns - 1 - i) if rev else (lambda i: i)

    def pspec(c0):
        return pl.BlockSpec((None, rows, LANE), lambda b, h, i: (b, seq(i), c0 + h))

    vec = pl.BlockSpec((None, 1, LANE), lambda b, h, i: (h, 0, 0))
    in_specs = [pspec(col["hg_q"]), pspec(zcol), pspec(col["hg_i"]), vec, vec,
                pl.BlockSpec(stat.shape, lambda b, h, i: (0, 0)),
                pl.BlockSpec(masks.shape, lambda b, h, i: (0, 0, 0))]
    args = [p3, p3, p3, llb, l1m, stat, masks]
    if rev:
        in_specs += [pl.BlockSpec((None, rows, LANE), lambda b, h, i: (b, seq(i), h)),
                     pspec(col["hg_g"]),
                     pl.BlockSpec((1, LANE), lambda b, h, i: (0, 0))]
        args += [o_fwd, p3, norm_g]
    out_shape = jax.ShapeDtypeStruct((bsz, s, heads * LANE), BF16 if rev else F32)
    out_specs = pl.BlockSpec((None, rows, LANE), lambda b, h, i: (b, seq(i), h))
    if side is not None:
        side3, side_in, side_out, side_shape = _side_plan(
            side, bsz * heads * ns, lambda b, h, i: ((b * heads + h) * ns + i, 0, 0))
        in_specs, args = [side_in] + in_specs, [side3] + args
        out_specs, out_shape = [side_out, out_specs], [side_shape, out_shape]
    res = pl.pallas_call(
        functools.partial(_hgrn2_kernel, rev=rev, nchunk=nchunk, unroll=unroll, epilogue=rev,
                          scale=HEAD_DIM ** -0.5, side=side is not None),
        grid=(bsz, heads, ns),
        in_specs=in_specs,
        out_specs=out_specs,
        out_shape=out_shape,
        scratch_shapes=[pltpu.VMEM((HEAD_DIM, HEAD_DIM), F32)],
        compiler_params=_cparams(("parallel", "parallel", "arbitrary")),
        name="hgrn2_bwd" if rev else "hgrn2_fwd",
    )(*args)
    return (res, None) if side is None else _side_result(res, side)


def _gd_prep_kernel(x_ref, w_ref, o_ref, xp_ref, *, s, tile, n_qk, n_q, scale_q):
    j = pl.program_id(1)
    halo = 8
    xp_ref[0:halo, :] = jnp.zeros((halo, LANE), F32)
    xp_ref[halo + s:2 * halo + s, :] = jnp.zeros((halo, LANE), F32)
    xp_ref[halo:halo + s, :] = x_ref[...]
    w = w_ref[...]
    inv_scale = jnp.where(j < n_q, scale_q, 1.0)

    def tile_step(ti, carry):
        r0 = pl.multiple_of(ti * tile, tile)
        win = xp_ref[pl.ds(r0, tile + 2 * halo), :]
        acc = jnp.zeros((tile, LANE), F32)
        for tap in range(CONV_K):
            off = halo + tap - CONV_K // 2
            acc = acc + w[tap:tap + 1, :] * win[off:off + tile, :]
        y = acc * _sigmoid(acc)
        inv = lax.rsqrt(jnp.sum(y * y, axis=-1, keepdims=True) + RMS_EPS) * inv_scale
        o_ref[pl.ds(r0, tile), :] = y * jnp.where(j < n_qk, inv, 1.0)
        return carry

    lax.fori_loop(0, s // tile, tile_step, 0)


def _gd_prep(p3, conv_w, *, col0, heads, tile=1024):
    bsz, s, _ = p3.shape
    tile = _pick(s, tile)
    nblk = 3 * heads
    return pl.pallas_call(
        functools.partial(_gd_prep_kernel, s=s, tile=tile, n_qk=2 * heads, n_q=heads, scale_q=HEAD_DIM ** -0.5),
        grid=(bsz, nblk),
        in_specs=[pl.BlockSpec((None, s, LANE), lambda b, j: (b, 0, col0 + j)),
                  pl.BlockSpec((CONV_K, LANE), lambda b, j: (0, j))],
        out_specs=pl.BlockSpec((None, s, LANE), lambda b, j: (b, 0, j)),
        out_shape=jax.ShapeDtypeStruct((bsz, s, nblk * LANE), F32),
        scratch_shapes=[pltpu.VMEM((s + 16, LANE), F32)],
        compiler_params=_cparams(("parallel", "parallel")),
        name="gd_prep",
    )(p3, conv_w)


@functools.lru_cache(maxsize=None)
def _gd_static(c, rev):
    t = np.arange(c)[:, None]
    s = np.arange(c)[None, :]
    incl = (s >= t) if rev else (s <= t)
    strict = (s > t) if rev else (s < t)
    mats = [incl, strict, t == s, np.ones((c, c), bool), strict & ((t // 8) == (s // 8))]
    blk = 16
    while blk <= c:
        half = blk // 2
        same = (t // blk) == (s // blk)
        t2, s2 = (t % blk) >= half, (s % blk) >= half
        mats.append(same & ((~t2) & s2 if rev else t2 & (~s2)))
        blk *= 2
    return np.stack(mats).astype(np.float32)


def _gdn_kernel(*refs, rev, nchunk, unroll, epilogue, heads, hb, side):
    if side:
        side_in_ref, *rest = refs
        n_in = 11 if epilogue else 8
        _side_cast(side_in_ref, rest[n_in])
        refs = rest[:n_in] + rest[n_in + 1:]
    if epilogue:
        (q_ref, k_ref, v_ref, pg_ref, na_ref, dtb_ref, stat_ref, stat2_ref,
         of_ref, og_ref, ng_ref, o_ref, s_ref, gb_ref, bb_ref) = refs
    else:
        q_ref, k_ref, v_ref, pg_ref, na_ref, dtb_ref, stat_ref, stat2_ref, o_ref, s_ref, gb_ref, bb_ref = refs
    c = CHUNK
    ngroups = nchunk // unroll
    hg = pl.program_id(1)

    @pl.when(pl.program_id(2) == 0)
    def _():
        s_ref[...] = jnp.zeros_like(s_ref)

    incl = stat_ref[0]
    strict = stat_ref[1]
    eye = stat_ref[2]
    incl2 = stat2_ref[0]
    ones2 = stat2_ref[1]
    n_lvl = stat_ref.shape[0] - 5

    pg = pg_ref[...]
    glog = na_ref[...] * _softplus(pg + dtb_ref[...])
    beta = _sigmoid(pg)
    cum_all = jnp.concatenate(
        [_dot_split_lhs(incl2, glog[ci * c:(ci + 1) * c]) for ci in range(nchunk)], axis=0)
    rsel = lax.broadcasted_iota(jnp.int32, (LANE, LANE), 0)
    d = 1 if rev else 0
    for hh in range(hb):
        h = hg * hb + hh
        gb_ref[hh] = _dot_split_rhs(cum_all, jnp.where(rsel == d * heads + h, 1.0, 0.0).astype(BF16))
        bb_ref[hh] = _dot(beta.astype(BF16), jnp.where(rsel == (2 + d) * heads + h, 1.0, 0.0).astype(BF16))

    d8 = stat_ref[4]
    edge_row = 0 if rev else c - 1

    def group_step(gi, carry):
        rows_list = _group_rows(gi, ngroups, unroll, rev)
        items = [(rows, hh) for rows in rows_list for hh in range(hb)]
        ni = len(items)
        rng = range(ni)
        lanes = [slice(hh * LANE, (hh + 1) * LANE) for _, hh in items]
        q = [q_ref[rows, lanes[i]] for i, (rows, _) in enumerate(items)]
        k = [k_ref[rows, lanes[i]] for i, (rows, _) in enumerate(items)]
        v = [v_ref[rows, lanes[i]] for i, (rows, _) in enumerate(items)]
        cum = [gb_ref[hh, rows, :] for rows, hh in items]
        bb = [bb_ref[hh, rows, :] for rows, hh in items]
        cum_row = [_dot_split_lhs(ones2, cum[i][:, 0:c] * eye) for i in rng]
        decay = [jnp.exp(jnp.where(incl > 0, cum[i][:, 0:c] - cum_row[i], -jnp.inf)) for i in rng]
        ecum = [jnp.exp(cum[i]) for i in rng]
        kb = [k[i] * bb[i] for i in rng]
        kbf = [k[i].astype(BF16) for i in rng]
        a = [strict * _dot_nt(kb[i].astype(BF16), kbf[i]) * decay[i] for i in rng]
        qk = [(_dot_nt(q[i].astype(BF16), kbf[i]) * decay[i]).astype(BF16) for i in rng]
        a0 = [a[i] * d8 for i in rng]
        a0b = [a0[i].astype(BF16) for i in rng]
        a2b = [_dot(a0b[i], a0b[i]).astype(BF16) for i in rng]
        xm = [eye - a0[i] for i in rng]
        ym = [xm[i] + _dot(xm[i].astype(BF16), a2b[i]) for i in rng]
        a4b = [_dot(a2b[i], a2b[i]).astype(BF16) for i in rng]
        tm = [ym[i] + _dot(ym[i].astype(BF16), a4b[i]) for i in rng]
        for lvl in range(n_lvl):
            off = stat_ref[5 + lvl]
            tmb = [tm[i].astype(BF16) for i in rng]
            mid = [_dot(tmb[i], (a[i] * off).astype(BF16)).astype(BF16) for i in rng]
            tm = [tm[i] - _dot(mid[i], tmb[i]) for i in rng]
        tmb = [tm[i].astype(BF16) for i in rng]
        u = [_dot(tmb[i], (v[i] * bb[i]).astype(BF16)) for i in rng]
        w = [_dot(tmb[i], (kb[i] * ecum[i]).astype(BF16)).astype(BF16) for i in rng]
        edge = [cum[i][edge_row:edge_row + 1] for i in rng]
        qd = [(q[i] * ecum[i]).astype(BF16) for i in rng]
        kd = [(k[i] * jnp.exp(edge[i] - cum[i])).astype(BF16) for i in rng]
        de = [jnp.exp(edge[i]) for i in rng]
        st = [s_ref[hh] for hh in range(hb)]
        heads_rng = range(hb)
        for ui, rows in enumerate(rows_list):
            idx = [ui * hb + hh for hh in heads_rng]
            stb = [st[hh].astype(BF16) for hh in heads_rng]
            ws = [_dot(w[idx[hh]], stb[hh]) for hh in heads_rng]
            oq = [_dot(qd[idx[hh]], stb[hh]) for hh in heads_rng]
            vnb = [(u[idx[hh]] - ws[hh]).astype(BF16) for hh in heads_rng]
            dst = [_dot_tn(kd[idx[hh]], vnb[hh]) for hh in heads_rng]
            o = [oq[hh] + _dot(qk[idx[hh]], vnb[hh]) for hh in heads_rng]
            st = [st[hh] * de[idx[hh]] + dst[hh] for hh in heads_rng]
            for hh in heads_rng:
                ln = slice(hh * LANE, (hh + 1) * LANE)
                if epilogue:
                    tot = o[hh] + of_ref[rows, ln]
                    y = tot * lax.rsqrt(jnp.mean(tot * tot, axis=-1, keepdims=True) + RMS_EPS) * ng_ref[...]
                    gate = og_ref[rows, ln]
                    o_ref[rows, ln] = (y * gate * _sigmoid(gate)).astype(o_ref.dtype)
                else:
                    o_ref[rows, ln] = o[hh]
        for hh in heads_rng:
            s_ref[hh] = st[hh]
        return carry

    lax.fori_loop(0, ngroups, group_step, 0)


def _gdn(qkv, pg, p3, o_fwd, neg_a, dtb, norm_g, *, rev, heads, gcol, rows=512, unroll=4, hb=8, side=None):
    bsz, s, _ = qkv.shape
    rows = _pick(s, rows)
    ns = s // rows
    nchunk = rows // CHUNK
    unroll = math.gcd(unroll, nchunk)
    hb = math.gcd(hb, heads)
    stat_np = _gd_static(CHUNK, rev)
    stat = jnp.asarray(stat_np, F32)
    stat2 = jnp.asarray(np.stack([np.concatenate([stat_np[i], stat_np[i]], axis=1) for i in (0, 3)]), BF16)
    seq = (lambda i: ns - 1 - i) if rev else (lambda i: i)
    wide = hb * LANE

    def qspec(c0):
        return pl.BlockSpec((None, rows, wide), lambda b, h, i: (b, seq(i), c0 // hb + h))

    vec = pl.BlockSpec((1, LANE), lambda b, h, i: (0, 0))
    in_specs = [qspec(0), qspec(heads), qspec(2 * heads),
                pl.BlockSpec((None, rows, LANE), lambda b, h, i: (b, seq(i), 0)),
                vec, vec, pl.BlockSpec(stat.shape, lambda b, h, i: (0, 0, 0)),
                pl.BlockSpec(stat2.shape, lambda b, h, i: (0, 0, 0))]
    args = [qkv, qkv, qkv, pg, neg_a, dtb, stat, stat2]
    if rev:
        in_specs += [qspec(0), qspec(gcol), vec]
        args += [o_fwd, p3, norm_g]
    out_shape = jax.ShapeDtypeStruct((bsz, s, heads * LANE), BF16 if rev else F32)
    out_spec = qspec(0)
    if side is not None:
        ng = heads // hb
        side3, side_in, side_out, side_shape = _side_plan(
            side, bsz * ng * ns, lambda b, h, i: ((b * ng + h) * ns + i, 0, 0))
        in_specs, args = [side_in] + in_specs, [side3] + args
        out_spec, out_shape = [side_out, out_spec], [side_shape, out_shape]
    res = pl.pallas_call(
        functools.partial(_gdn_kernel, rev=rev, nchunk=nchunk, unroll=unroll, epilogue=rev, heads=heads, hb=hb,
                          side=side is not None),
        grid=(bsz, heads // hb, ns),
        in_specs=in_specs,
        out_specs=out_spec,
        out_shape=out_shape,
        scratch_shapes=[pltpu.VMEM((hb, HEAD_DIM, HEAD_DIM), F32),
                        pltpu.VMEM((hb, rows, LANE), F32), pltpu.VMEM((hb, rows, LANE), F32)],
        compiler_params=_cparams(("parallel", "parallel", "arbitrary")),
        name="gdn_bwd" if rev else "gdn_fwd",
    )(*args)
    return (res, None) if side is None else _side_result(res, side)


def _pad_cols(w, n):
    return jnp.pad(w, ((0, 0),) * (w.ndim - 1) + ((0, n - w.shape[-1]),))


def kernel(x, ln_in_g, ln_in_b, w_in, hg_lb_param, gd_conv_w, gd_a_log, gd_dt_bias, hg_norm_g, gd_norm_g, w_out,
           ln1_g, ln1_b, ffn_w_gate, ffn_w_up, ffn_w_down, moe_router, moe_w_gate, moe_w_up, moe_w_down,
           ln2_g, ln2_b):
    bsz, s, d = x.shape
    depth = w_in.shape[0]
    m = bsz * s
    half = d // 2
    heads = half // HEAD_DIM
    alpha = (2 * depth) ** 0.25
    n_main = 8 * half
    n_gate = N_GATE_COLS * heads
    names = ("hg_q", "hg_f_fwd", "hg_f_bwd", "hg_i", "hg_g", "gd_q", "gd_k", "gd_v", "gd_g")
    col = {nm: i * heads for i, nm in enumerate(names)}

    lb = jnp.cumsum(jax.nn.softmax(hg_lb_param.astype(F32), axis=0), axis=0)
    lb = lb - lb[0:1]
    log_lb = jnp.log(lb).reshape(depth, 2, heads, 1, LANE)
    log_1m_lb = jnp.log1p(-lb).reshape(depth, 2, heads, 1, LANE)

    f_dense = ffn_w_gate.shape[-1]
    f_pad = -(-f_dense // 1024) * 1024
    tasks = []
    for l in range(depth):
        due = 3 * l + 2
        tasks.append((due, ("w_out", l), w_out[l], d))
        if l % 2 == 0:
            tasks += [(due, ("ffn_gate", l // 2), ffn_w_gate[l // 2], f_pad),
                      (due, ("ffn_up", l // 2), ffn_w_up[l // 2], f_pad)]
        else:
            tasks += [(due, (nm, l // 2), wt[l // 2], wt.shape[-1])
                      for nm, wt in (("moe_gate", moe_w_gate), ("moe_up", moe_w_up), ("moe_down", moe_w_down))]
    tasks.sort(key=lambda t: (t[0], -t[2].size))
    cast_bf16 = {}

    def ride_along(carrier, nsteps):
        for t in tasks:
            due, _, w, _ = t
            if carrier <= due and (w.size // w.shape[-1]) % (16 * nsteps) == 0:
                tasks.remove(t)
                return t
        return None

    def carried(fn, carrier, nsteps, *args, **kwargs):
        t = ride_along(carrier, nsteps)
        if t is None:
            return fn(*args, **kwargs)[0]
        out, cast_bf16[t[1]] = fn(*args, side=(t[2], t[3]), **kwargs)
        return out

    def bf16_weight(key, w, out_cols):
        if key in cast_bf16:
            return cast_bf16[key]
        return _pad_cols(w, out_cols).astype(BF16)

    hg_steps = bsz * heads * (s // _pick(s, 2048))
    gd_steps = bsz * (heads // math.gcd(8, heads)) * (s // _pick(s, 512))

    xf, xb = _layer_norm(x.reshape(m, d), None, ln_in_g, ln_in_b)
    for l in range(depth):
        w_l = w_in[l]
        w_main = jnp.concatenate([w_l[:, :n_main], w_l[:, n_main + n_gate:]], axis=1).astype(BF16)
        w_gate = _pad_cols(w_l[:, n_main:n_main + n_gate], LANE).astype(BF16)
        p3 = _matmul(xb, w_main, F32, tn=1024, name="in_proj").reshape(bsz, s, -1)
        pg = _matmul(xb, w_gate, F32, tn=LANE, name="gate_proj").reshape(bsz, s, LANE)

        o_f = carried(_hgrn2, 3 * l, hg_steps, p3, None, log_lb[l, 0], log_1m_lb[l, 0], None,
                      rev=False, heads=heads, col=col, d_model=d)
        hg_out = carried(_hgrn2, 3 * l + 1, hg_steps, p3, o_f, log_lb[l, 1], log_1m_lb[l, 1],
                         hg_norm_g[l].reshape(1, LANE).astype(F32), rev=True, heads=heads, col=col, d_model=d)

        qkv = _gd_prep(p3, gd_conv_w[l].astype(F32), col0=col["gd_q"], heads=heads)
        neg_a = jnp.zeros((1, LANE), F32).at[0, :2 * heads].set(-jnp.exp(gd_a_log[l].astype(F32)).reshape(-1))
        dtb = jnp.zeros((1, LANE), F32).at[0, :2 * heads].set(gd_dt_bias[l].astype(F32).reshape(-1))
        gn = gd_norm_g[l].reshape(1, LANE).astype(F32)
        g_f = carried(_gdn, 3 * l + 2, gd_steps, qkv, pg, None, None, neg_a, dtb, None,
                      rev=False, heads=heads, gcol=col["gd_g"])
        gd_out = _gdn(qkv, pg, p3, g_f, neg_a, dtb, gn, rev=True, heads=heads, gcol=col["gd_g"])[0]

        mix = _matmul_cat(hg_out.reshape(m, half), gd_out.reshape(m, half), bf16_weight(("w_out", l), w_out[l], d),
                          F32, name="out_proj")
        xf, xb = _layer_norm(xf, mix, ln1_g[l], ln1_b[l], alpha=alpha)

        j = l // 2
        if l % 2 == 0:
            hdn = _swiglu(xb, bf16_weight(("ffn_gate", j), ffn_w_gate[j], f_pad),
                          bf16_weight(("ffn_up", j), ffn_w_up[j], f_pad))
            wd = jnp.pad(ffn_w_down[j], ((0, f_pad - f_dense), (0, 0))).astype(BF16)
            ffn = _matmul(hdn, wd, F32, tn=1024, tk=2816, name="down_proj")
            xf, xb = _layer_norm(xf, ffn, ln2_g[l], ln2_b[l], alpha=alpha)
        else:
            n_experts = moe_router.shape[-1]
            top2 = _router_top2(xf, moe_router[j])
            dest, tile_expert, n_used, n_tiles = _route_plan(top2, n_experts, MOE_TILE)
            xg = _dispatch(xf, dest, n_tiles * MOE_TILE)
            wts = [bf16_weight((nm, j), wt[j], wt.shape[-1])
                   for nm, wt in (("moe_gate", moe_w_gate), ("moe_up", moe_w_up), ("moe_down", moe_w_down))]
            yg = _grouped_experts(xg, *wts, tile_expert, n_used)
            xf, xb = _combine_layer_norm(xf, yg, top2, dest, ln2_g[l], ln2_b[l], alpha)
    return xf.reshape(bsz, s, d)
```
